```python
import math
import jax, jax.numpy as jnp
from jax import lax
import numpy as np

D_MODEL = 2048
BATCH = 1
SEQ = 16384
DEPTH = 4

N_MIXERS = 3
D_FF = 5632
PLE_DIM = 256
SC_WIDTH = 3
CF_WIDTH = 31
HY_SHORT_WIDTH = 3
HY_EMB_DIM = 33
HY_FILTER_ORDER = 64
HY_FAST_DECAY_PCT = 0.3
HY_SLOW_DECAY_PCT = 1.5
HY_TARGET = 1e-2
RMS_EPS = 1e-6
LN_EPS = 1e-5
N_A = (DEPTH + 2) // 3
N_B = (DEPTH + 1) // 3
N_C = DEPTH // 3

kernel_name = "hybrid_conv_encoder_block"


def rmsnorm(x, g):
    x32 = x.astype(jnp.float32)
    y = x32 * lax.rsqrt(jnp.mean(x32 * x32, axis=-1, keepdims=True) + RMS_EPS) * g.astype(jnp.float32)
    return y.astype(x.dtype)


def layernorm(x, g, b):
    x32 = x.astype(jnp.float32)
    mu = jnp.mean(x32, axis=-1, keepdims=True)
    xc = x32 - mu
    y = xc * lax.rsqrt(jnp.mean(xc * xc, axis=-1, keepdims=True) + LN_EPS) * g.astype(jnp.float32) + b.astype(jnp.float32)
    return y.astype(x.dtype)


def dwconv(x, w):
    k = w.shape[0]
    return lax.conv_general_dilated(
        x, w[:, None, :].astype(x.dtype), window_strides=(1,),
        padding=[(k // 2, k // 2)], dimension_numbers=("NWC", "WIO", "NWC"),
        feature_group_count=x.shape[-1])


def swiglu(h, w_gate, w_up, w_down):
    return (jax.nn.silu(h @ w_gate) * (h @ w_up)) @ w_down


def short_conv_mixer(h, w_in, w_conv, w_out):
    b, c, v = jnp.split(h @ w_in, 3, axis=-1)
    return (b * dwconv(c * v, w_conv)) @ w_out


def conformer_conv_mixer(h, w_pw1, b_pw1, w_dw, b_dw, ln_g, ln_b, w_pw2, b_pw2):
    a = jax.nn.glu(h @ w_pw1 + b_pw1, axis=-1)
    a = dwconv(a, w_dw) + b_dw
    a = jax.nn.silu(layernorm(a, ln_g, ln_b))
    return a @ w_pw2 + b_pw2


def hyena_positional_features(seq_len):
    t = jnp.linspace(0.0, 1.0, seq_len, dtype=jnp.float32)[:, None]
    bands = (HY_EMB_DIM - 1) // 2
    w = (2.0 * math.pi / seq_len) * jnp.arange(seq_len, dtype=jnp.float32)[:, None]
    f = jnp.linspace(1e-4, bands - 1, bands, dtype=jnp.float32)[None, :]
    fw = f * w
    z = jnp.concatenate([t, jnp.cos(fw), -jnp.sin(fw)], axis=-1)
    return z, t


def hyena_filter(seq_len, d, w1, b1, w2, b2, w3, b3, w4, freq):
    z, t = hyena_positional_features(seq_len)
    f32 = lambda a: a.astype(jnp.float32)
    fr = f32(freq)
    h = jnp.sin(fr * (z @ f32(w1) + f32(b1)))
    h = jnp.sin(fr * (h @ f32(w2) + f32(b2)))
    h = jnp.sin(fr * (h @ f32(w3) + f32(b3)))
    h = h @ f32(w4)
    deltas = jnp.abs(jnp.linspace(math.log(HY_TARGET) / HY_FAST_DECAY_PCT,
                                  math.log(HY_TARGET) / HY_SLOW_DECAY_PCT, d, dtype=jnp.float32))
    decay = jnp.exp(-t * deltas)
    h_fwd = h[:, :d] * decay
    h_bwd = h[:, d:] * decay
    k = jnp.concatenate([h_fwd, jnp.zeros((1, d), jnp.float32), h_bwd[:0:-1]], axis=0)
    return k * lax.rsqrt(jnp.sum(k * k, axis=0, keepdims=True))


def bidirectional_fft_conv(u, k):
    seq_len = u.shape[1]
    uf = jnp.fft.rfft(u.astype(jnp.float32), n=2 * seq_len, axis=1)
    kf = jnp.fft.rfft(k, n=2 * seq_len, axis=0)
    return jnp.fft.irfft(uf * kf[None], n=2 * seq_len, axis=1)[:, :seq_len]


def hyena_mixer(h, w_in, w_short, b_short, f_w1, f_b1, f_w2, f_b2, f_w3, f_b3, f_w4, freq, skip, w_out):
    seq_len, d = h.shape[1], h.shape[2]
    u = dwconv(h @ w_in, w_short) + b_short
    x0, x1, v = jnp.split(u, 3, axis=-1)
    z = v * x1
    k = hyena_filter(seq_len, d, f_w1, f_b1, f_w2, f_b2, f_w3, f_b3, f_w4, freq)
    y = bidirectional_fft_conv(z, k) + z.astype(jnp.float32) * skip.astype(jnp.float32)
    return (y.astype(h.dtype) * x0) @ w_out


def setup_inputs(seed: int = 0) -> dict:
    key = jax.random.key(seed)
    ks = iter(jax.random.split(key, 64))
    D, F = D_MODEL, D_FF

    def w(shape, fan_in):
        return jax.random.normal(next(ks), shape, jnp.float32) * (fan_in ** -0.5)

    def gain(shape):
        return 1.0 + 0.01 * jax.random.normal(next(ks), shape, jnp.float32)

    def bias(shape):
        return 0.01 * jax.random.normal(next(ks), shape, jnp.float32)

    return {
        "x": jax.random.normal(next(ks), (BATCH, SEQ, D), jnp.float32),
        "p": jax.random.normal(next(ks), (DEPTH, BATCH, SEQ, PLE_DIM), jnp.float32),
        "norm_ffn1": gain((DEPTH, D)),
        "ffn1_w_gate": w((DEPTH, D, F), D),
        "ffn1_w_up": w((DEPTH, D, F), D),
        "ffn1_w_down": w((DEPTH, F, D), F),
        "norm_mix": gain((DEPTH, D)),
        "norm_ffn2": gain((DEPTH, D)),
        "ffn2_w_gate": w((DEPTH, D, F), D),
        "ffn2_w_up": w((DEPTH, D, F), D),
        "ffn2_w_down": w((DEPTH, F, D), F),
        "norm_ple": gain((DEPTH, D)),
        "ple_w_gate": w((DEPTH, D, D), D),
        "ple_w_up": w((DEPTH, PLE_DIM, D), PLE_DIM),
        "sc_w_in": w((N_A, D, 3 * D), D),
        "sc_conv": w((N_A, SC_WIDTH, D), SC_WIDTH),
        "sc_w_out": w((N_A, D, D), D),
        "cf_w_pw1": w((N_B, D, 2 * D), D),
        "cf_b_pw1": bias((N_B, 2 * D)),
        "cf_dw": w((N_B, CF_WIDTH, D), CF_WIDTH),
        "cf_b_dw": bias((N_B, D)),
        "cf_ln_g": gain((N_B, D)),
        "cf_ln_b": bias((N_B, D)),
        "cf_w_pw2": w((N_B, D, D), D),
        "cf_b_pw2": bias((N_B, D)),
        "hy_w_in": w((N_C, D, 3 * D), D),
        "hy_short": w((N_C, HY_SHORT_WIDTH, 3 * D), HY_SHORT_WIDTH),
        "hy_b_short": bias((N_C, 3 * D)),
        "hy_f_w1": w((N_C, HY_EMB_DIM, HY_FILTER_ORDER), HY_EMB_DIM),
        "hy_f_b1": bias((N_C, HY_FILTER_ORDER)),
        "hy_f_w2": w((N_C, HY_FILTER_ORDER, HY_FILTER_ORDER), HY_FILTER_ORDER),
        "hy_f_b2": bias((N_C, HY_FILTER_ORDER)),
        "hy_f_w3": w((N_C, HY_FILTER_ORDER, HY_FILTER_ORDER), HY_FILTER_ORDER),
        "hy_f_b3": bias((N_C, HY_FILTER_ORDER)),
        "hy_f_w4": w((N_C, HY_FILTER_ORDER, 2 * D), HY_FILTER_ORDER),
        "hy_freq": gain((N_C, HY_FILTER_ORDER)),
        "hy_skip": w((N_C, D), 1),
        "hy_w_out": w((N_C, D, D), D),
        "norm_final": gain((D,)),
    }


def reference(x, p, norm_ffn1, ffn1_w_gate, ffn1_w_up, ffn1_w_down, norm_mix,
              norm_ffn2, ffn2_w_gate, ffn2_w_up, ffn2_w_down, norm_ple, ple_w_gate, ple_w_up,
              sc_w_in, sc_conv, sc_w_out,
              cf_w_pw1, cf_b_pw1, cf_dw, cf_b_dw, cf_ln_g, cf_ln_b, cf_w_pw2, cf_b_pw2,
              hy_w_in, hy_short, hy_b_short, hy_f_w1, hy_f_b1, hy_f_w2, hy_f_b2,
              hy_f_w3, hy_f_b3, hy_f_w4, hy_freq, hy_skip, hy_w_out, norm_final):
    for i in range(DEPTH):
        m, j = i % N_MIXERS, i // N_MIXERS
        x = x + 0.5 * swiglu(rmsnorm(x, norm_ffn1[i]), ffn1_w_gate[i], ffn1_w_up[i], ffn1_w_down[i])
        h = rmsnorm(x, norm_mix[i])
        if m == 0:
            y = short_conv_mixer(h, sc_w_in[j], sc_conv[j], sc_w_out[j])
        elif m == 1:
            y = conformer_conv_mixer(h, cf_w_pw1[j], cf_b_pw1[j], cf_dw[j], cf_b_dw[j],
                                     cf_ln_g[j], cf_ln_b[j], cf_w_pw2[j], cf_b_pw2[j])
        else:
            y = hyena_mixer(h, hy_w_in[j], hy_short[j], hy_b_short[j], hy_f_w1[j], hy_f_b1[j],
                            hy_f_w2[j], hy_f_b2[j], hy_f_w3[j], hy_f_b3[j], hy_f_w4[j],
                            hy_freq[j], hy_skip[j], hy_w_out[j])
        x = x + y
        x = x + 0.5 * swiglu(rmsnorm(x, norm_ffn2[i]), ffn2_w_gate[i], ffn2_w_up[i], ffn2_w_down[i])
        gate = jax.nn.sigmoid(rmsnorm(x, norm_ple[i]) @ ple_w_gate[i])
        x = x + (p[i] @ ple_w_up[i]) * gate
    return rmsnorm(x, norm_final)
```

```python
import functools
import math

import jax
import jax.numpy as jnp
from jax import lax
from jax.experimental import pallas as pl
from jax.experimental.pallas import tpu as pltpu

BF = jnp.bfloat16
F32 = jnp.float32

RMS_EPS = 1e-6
LN_EPS = 1e-5
HY_FAST_DECAY_PCT = 0.3
HY_SLOW_DECAY_PCT = 1.5
HY_TARGET = 1e-2

V7X_LANES = 128
V7X_BF16_SUBLANES = 16
HALO = V7X_BF16_SUBLANES
MIB = 1024 * 1024


def _dot(a, b):
    return jnp.dot(a, b, preferred_element_type=F32)


def _dot_hi(a, b):
    return jnp.dot(a, b, preferred_element_type=F32, precision=lax.Precision.HIGHEST)


def _rms(x, g):
    return x * lax.rsqrt(jnp.mean(x * x, axis=-1, keepdims=True) + RMS_EPS) * g


def _round_up(n, m):
    return (n + m - 1) // m * m


def _params(vmem_mib):
    return pltpu.CompilerParams(
        dimension_semantics=None, vmem_limit_bytes=int(vmem_mib * MIB))


def _resident(shape, index_map):
    return pl.BlockSpec(shape, index_map, pipeline_mode=pl.Buffered(1))


def _ffn_kernel(x_ref, g_ref, wg_ref, wu_ref, wd_ref, o_ref, h_ref):
    @pl.when(pl.program_id(1) == 0)
    def _():
        x = x_ref[...]
        h_ref[...] = _rms(x, g_ref[...]).astype(BF)
        o_ref[...] = x

    h = h_ref[...]
    a = _dot(h, wg_ref[...])
    u = _dot(h, wu_ref[...])
    act = (a * jax.nn.sigmoid(a) * u * 0.5).astype(BF)
    o_ref[...] += _dot(act, wd_ref[...])


def _ffn(x, g, wg, wu, wd, layer, *, tm, tf):
    seq, d = x.shape
    f = wg.shape[-1]
    return pl.pallas_call(
        _ffn_kernel,
        grid=(seq // tm, f // tf),
        in_specs=[
            pl.BlockSpec((tm, d), lambda i, j: (i, 0)),
            pl.BlockSpec((None, 1, d), lambda i, j: (layer, 0, 0)),
            pl.BlockSpec((None, d, tf), lambda i, j: (layer, 0, j)),
            pl.BlockSpec((None, d, tf), lambda i, j: (layer, 0, j)),
            pl.BlockSpec((None, tf, d), lambda i, j: (layer, j, 0)),
        ],
        out_specs=pl.BlockSpec((tm, d), lambda i, j: (i, 0)),
        out_shape=jax.ShapeDtypeStruct((seq, d), F32),
        scratch_shapes=[pltpu.VMEM((tm, d), BF)],
        compiler_params=_params(48),
        name="ffn",
    )(x, g, wg, wu, wd)


def _ple_kernel(x_ref, p_ref, g_ref, wgate_ref, wup_ref, gf_ref, o_ref, *, final):
    x = x_ref[...]
    h = _rms(x, g_ref[...]).astype(BF)
    gate = jax.nn.sigmoid(_dot(h, wgate_ref[...]))
    up = _dot(p_ref[...].astype(BF), wup_ref[...])
    y = x + up * gate
    if final:
        y = _rms(y, gf_ref[...])
    o_ref[...] = y


def _ple(x, p, g, wgate, wup, gf, layer, *, tm, final):
    seq, d = x.shape
    pd = p.shape[-1]
    return pl.pallas_call(
        functools.partial(_ple_kernel, final=final),
        grid=(seq // tm,),
        in_specs=[
            pl.BlockSpec((tm, d), lambda i: (i, 0)),
            pl.BlockSpec((None, tm, pd), lambda i: (layer, i, 0)),
            pl.BlockSpec((None, 1, d), lambda i: (layer, 0, 0)),
            _resident((None, d, d), lambda i: (layer, 0, 0)),
            _resident((None, pd, d), lambda i: (layer, 0, 0)),
            pl.BlockSpec((1, d), lambda i: (0, 0)),
        ],
        out_specs=pl.BlockSpec((tm, d), lambda i: (i, 0)),
        out_shape=jax.ShapeDtypeStruct((seq, d), F32),
        compiler_params=_params(40),
        name="ple",
    )(x, p, g, wgate, wup, gf)


def _halo_specs(tm, d, seq):
    per = tm // HALO
    last = seq // HALO - 1
    return [
        pl.BlockSpec((HALO, d), lambda i, j: (jnp.maximum(i * per - 1, 0), 0)),
        pl.BlockSpec((tm, d), lambda i, j: (i, 0)),
        pl.BlockSpec((HALO, d), lambda i, j: (jnp.minimum((i + 1) * per, last), 0)),
    ]


def _fill_normed(h_ref, xp_ref, x_ref, xn_ref, g, tm):
    h_ref[pl.ds(0, HALO), :] = _rms(xp_ref[...], g).astype(BF)
    h_ref[pl.ds(HALO, tm), :] = _rms(x_ref[...], g).astype(BF)
    h_ref[pl.ds(HALO + tm, HALO), :] = _rms(xn_ref[...], g).astype(BF)


def _in_sequence(tm, seq):
    row = pl.program_id(0) * tm - HALO + lax.broadcasted_iota(jnp.int32, (tm + 2 * HALO, 1), 0)
    return jnp.logical_and(row >= 0, row < seq)


def _dwconv(src_ref, taps, tm):
    k = taps.shape[0]
    acc = taps[0:1] * src_ref[pl.ds(HALO - k // 2, tm), :]
    for t in range(1, k):
        acc = acc + taps[t:t + 1] * src_ref[pl.ds(HALO - k // 2 + t, tm), :]
    return acc


def _sc_kernel(xp_ref, x_ref, xn_ref, g_ref, wb_ref, wc_ref, wv_ref, taps_ref, wo_ref,
               o_ref, h_ref, cv_ref, *, tm, seq):
    @pl.when(pl.program_id(1) == 0)
    def _():
        _fill_normed(h_ref, xp_ref, x_ref, xn_ref, g_ref[...], tm)
        o_ref[...] = x_ref[...]

    h = h_ref[...]
    cv = _dot(h, wc_ref[...]) * _dot(h, wv_ref[...])
    cv_ref[...] = jnp.where(_in_sequence(tm, seq), cv, 0.0)
    conv = _dwconv(cv_ref, taps_ref[...], tm)
    b = _dot(h_ref[pl.ds(HALO, tm), :], wb_ref[...])
    o_ref[...] += _dot((b * conv).astype(BF), wo_ref[...])


def _short_conv_mixer(x, g, w_in, taps, w_out, layer, j, *, tm, tn):
    seq, d = x.shape
    nd = d // tn
    kw = taps.shape[1]
    return pl.pallas_call(
        functools.partial(_sc_kernel, tm=tm, seq=seq),
        grid=(seq // tm, nd),
        in_specs=_halo_specs(tm, d, seq) + [
            pl.BlockSpec((None, 1, d), lambda i, c: (layer, 0, 0)),
            pl.BlockSpec((None, d, tn), lambda i, c: (j, 0, c)),
            pl.BlockSpec((None, d, tn), lambda i, c: (j, 0, nd + c)),
            pl.BlockSpec((None, d, tn), lambda i, c: (j, 0, 2 * nd + c)),
            pl.BlockSpec((None, kw, tn), lambda i, c: (j, 0, c)),
            pl.BlockSpec((None, tn, d), lambda i, c: (j, c, 0)),
        ],
        out_specs=pl.BlockSpec((tm, d), lambda i, c: (i, 0)),
        out_shape=jax.ShapeDtypeStruct((seq, d), F32),
        scratch_shapes=[pltpu.VMEM((tm + 2 * HALO, d), BF),
                        pltpu.VMEM((tm + 2 * HALO, tn), F32)],
        compiler_params=_params(48),
        name="short_conv",
    )(x, x, x, g, w_in, w_in, w_in, taps, w_out)


def _cf_kernel(xp_ref, x_ref, xn_ref, g_ref, wa_ref, wg_ref, ba_ref, bg_ref, taps_ref, bdw_ref,
               lng_ref, lnb_ref, w2_ref, b2_ref, o_ref, h_ref, a_ref, conv_ref, *, tm, tn, seq):
    c = pl.program_id(1)
    nd = pl.num_programs(1)

    @pl.when(c == 0)
    def _():
        _fill_normed(h_ref, xp_ref, x_ref, xn_ref, g_ref[...], tm)

    h = h_ref[...]
    a = (_dot(h, wa_ref[...]) + ba_ref[...]) * jax.nn.sigmoid(_dot(h, wg_ref[...]) + bg_ref[...])
    a_ref[...] = jnp.where(_in_sequence(tm, seq), a, 0.0)
    conv_ref[c] = _dwconv(a_ref, taps_ref[...], tm) + bdw_ref[...]

    @pl.when(c == nd - 1)
    def _():
        d = conv_ref.shape[0] * tn
        chunks = conv_ref.shape[0]
        total = jnp.zeros((tm, 1), F32)
        for k in range(chunks):
            total = total + jnp.sum(conv_ref[k], axis=-1, keepdims=True)
        mu = total / d
        sq = jnp.zeros((tm, 1), F32)
        for k in range(chunks):
            xc = conv_ref[k] - mu
            sq = sq + jnp.sum(xc * xc, axis=-1, keepdims=True)
        rstd = lax.rsqrt(sq / d + LN_EPS)
        o_ref[...] = x_ref[...] + b2_ref[...]
        for k in range(chunks):
            y = (conv_ref[k] - mu) * rstd * lng_ref[:, k * tn:(k + 1) * tn] + lnb_ref[:, k * tn:(k + 1) * tn]
            s = (y * jax.nn.sigmoid(y)).astype(BF)
            o_ref[...] += _dot(s, w2_ref[k * tn:(k + 1) * tn, :])


def _conformer_mixer(x, g, w1, b1, taps, bdw, lng, lnb, w2, b2, layer, j, *, tm, tn):
    seq, d = x.shape
    nd = d // tn
    kw = taps.shape[1]
    row = lambda i, c: (j, 0, 0)
    return pl.pallas_call(
        functools.partial(_cf_kernel, tm=tm, tn=tn, seq=seq),
        grid=(seq // tm, nd),
        in_specs=_halo_specs(tm, d, seq) + [
            pl.BlockSpec((None, 1, d), lambda i, c: (layer, 0, 0)),
            pl.BlockSpec((None, d, tn), lambda i, c: (j, 0, c)),
            pl.BlockSpec((None, d, tn), lambda i, c: (j, 0, nd + c)),
            pl.BlockSpec((None, 1, tn), lambda i, c: (j, 0, c)),
            pl.BlockSpec((None, 1, tn), lambda i, c: (j, 0, nd + c)),
            pl.BlockSpec((None, kw, tn), lambda i, c: (j, 0, c)),
            pl.BlockSpec((None, 1, tn), lambda i, c: (j, 0, c)),
            pl.BlockSpec((None, 1, d), row),
            pl.BlockSpec((None, 1, d), row),
            _resident((None, d, d), row),
            pl.BlockSpec((None, 1, d), row),
        ],
        out_specs=pl.BlockSpec((tm, d), lambda i, c: (i, 0)),
        out_shape=jax.ShapeDtypeStruct((seq, d), F32),
        scratch_shapes=[pltpu.VMEM((tm + 2 * HALO, d), BF),
                        pltpu.VMEM((tm + 2 * HALO, tn), F32),
                        pltpu.VMEM((nd, tm, tn), F32)],
        compiler_params=_params(48),
        name="conformer",
    )(x, x, x, g, w1, w1, b1, b1, taps, bdw, lng, lnb, w2, b2)


def _hy_in_kernel(xp_ref, x_ref, xn_ref, g_ref, w0_ref, w1_ref, w2_ref, t0_ref, t1_ref, t2_ref,
                  b0_ref, b1_ref, b2_ref, x0_ref, z_ref, h_ref, u_ref, *, tm, seq):
    @pl.when(pl.program_id(1) == 0)
    def _():
        _fill_normed(h_ref, xp_ref, x_ref, xn_ref, g_ref[...], tm)

    h = h_ref[...]
    inside = _in_sequence(tm, seq)

    def branch(w_ref, t_ref, b_ref):
        u_ref[...] = jnp.where(inside, _dot(h, w_ref[...]), 0.0)
        return _dwconv(u_ref, t_ref[...], tm) + b_ref[...]

    x0_ref[...] = branch(w0_ref, t0_ref, b0_ref).astype(BF)
    x1 = branch(w1_ref, t1_ref, b1_ref)
    z_ref[...] = (branch(w2_ref, t2_ref, b2_ref) * x1).astype(BF)


def _hyena_in(x, g, w_in, taps, bias, layer, j, *, tm, tn):
    seq, d = x.shape
    nd = d // tn
    kw = taps.shape[1]
    col = lambda k: (lambda i, c: (j, 0, k * nd + c))
    return pl.pallas_call(
        functools.partial(_hy_in_kernel, tm=tm, seq=seq),
        grid=(seq // tm, nd),
        in_specs=_halo_specs(tm, d, seq)
        + [pl.BlockSpec((None, 1, d), lambda i, c: (layer, 0, 0))]
        + [pl.BlockSpec((None, d, tn), col(k)) for k in range(3)]
        + [pl.BlockSpec((None, kw, tn), col(k)) for k in range(3)]
        + [pl.BlockSpec((None, 1, tn), col(k)) for k in range(3)],
        out_specs=[pl.BlockSpec((tm, tn), lambda i, c: (i, c))] * 2,
        out_shape=[jax.ShapeDtypeStruct((seq, d), BF)] * 2,
        scratch_shapes=[pltpu.VMEM((tm + 2 * HALO, d), BF),
                        pltpu.VMEM((tm + 2 * HALO, tn), F32)],
        compiler_params=_params(40),
        name="hyena_in",
    )(x, x, x, g, w_in, w_in, w_in, taps, taps, taps, bias, bias, bias)


def _hy_filter_kernel(fpad_ref, w1_ref, b1_ref, w2_ref, b2_ref, w3_ref, b3_ref, w4f_ref, w4b_ref,
                      fr_ref, dl_ref, hf_ref, hb_ref, ssq_ref, *, tr, seq, bands):
    i = pl.program_id(0)
    rowi = i * tr + lax.broadcasted_iota(jnp.int32, (tr, 1), 0)
    row = rowi.astype(F32)
    t = row / (seq - 1)
    arg = ((2.0 * math.pi / seq) * row) * fpad_ref[...]
    lane = lax.broadcasted_iota(jnp.int32, (1, fpad_ref.shape[1]), 1)
    feat = jnp.where(lane == 0, t,
                     jnp.where(lane <= bands, jnp.cos(arg),
                               jnp.where(lane <= 2 * bands, -jnp.sin(arg), 0.0)))
    fr = fr_ref[...]
    a = jnp.sin(fr * (_dot_hi(feat, w1_ref[...]) + b1_ref[...]))
    a = jnp.sin(fr * (_dot_hi(a, w2_ref[...]) + b2_ref[...]))
    a = jnp.sin(fr * (_dot_hi(a, w3_ref[...]) + b3_ref[...]))
    decay = jnp.exp(-t * dl_ref[...])
    hf = _dot_hi(a, w4f_ref[...]) * decay
    hb = jnp.where(rowi == 0, 0.0, _dot_hi(a, w4b_ref[...]) * decay)
    hf_ref[...] = hf.astype(BF)
    hb_ref[...] = hb.astype(BF)

    @pl.when(i == 0)
    def _():
        ssq_ref[...] = jnp.zeros_like(ssq_ref)

    ssq_ref[...] += jnp.sum(hf * hf + hb * hb, axis=0, keepdims=True)


def _hyena_filter(seq, d, fpad, w1, b1, w2, b2, w3, b3, w4, fr, deltas, bands, *, tr):
    wp = w1.shape[1]
    full = lambda shape: pl.BlockSpec(shape, lambda i: (0,) * len(shape))
    return pl.pallas_call(
        functools.partial(_hy_filter_kernel, tr=tr, seq=seq, bands=bands),
        grid=(seq // tr,),
        in_specs=[full(fpad.shape), full(w1.shape), full(b1.shape), full(w2.shape), full(b2.shape),
                  full(w3.shape), full(b3.shape),
                  pl.BlockSpec((wp, d), lambda i: (0, 0)), pl.BlockSpec((wp, d), lambda i: (0, 1)),
                  full(fr.shape), full(deltas.shape)],
        out_specs=[pl.BlockSpec((tr, d), lambda i: (i, 0)), pl.BlockSpec((tr, d), lambda i: (i, 0)),
                   pl.BlockSpec((1, d), lambda i: (0, 0))],
        out_shape=[jax.ShapeDtypeStruct((seq, d), BF), jax.ShapeDtypeStruct((seq, d), BF),
                   jax.ShapeDtypeStruct((1, d), F32)],
        compiler_params=_params(40),
        name="hyena_filter",
    )(fpad, w1, b1, w2, b2, w3, b3, w4, w4, fr, deltas)


def _dft_plan(seq):
    n = 2 * seq
    n2 = 256 if n >= 256 * 32 else 64
    n1 = n // n2
    h1 = n1 // 2
    k1p = _round_up(h1 + 1, V7X_BF16_SUBLANES)
    return n, n1, n2, h1, k1p


def _dft_tables(seq):
    n, n1, n2, h1, k1p = _dft_plan(seq)
    kin = _round_up(h1, V7X_LANES)
    khalf = _round_up(k1p, V7X_LANES)
    two_pi = 2.0 * math.pi

    k1 = jnp.arange(k1p, dtype=jnp.int32)
    pos = (jnp.arange(h1, dtype=jnp.int32)[None, :] * n2 + jnp.arange(n2, dtype=jnp.int32)[:, None])
    phase = (k1[None, :, None] * pos[:, None, :]) % n
    ang = phase.astype(F32) * (two_pi / n)
    fwd = jnp.concatenate([jnp.cos(ang), -jnp.sin(ang)], axis=1)
    fwd = jnp.pad(fwd, ((0, 0), (0, 0), (0, kin - h1))).astype(BF)

    weight = jnp.where(k1 == 0, 1.0, jnp.where(k1 < n1 // 2, 2.0, jnp.where(k1 == n1 // 2, 1.0, 0.0))) / n
    angt = jnp.swapaxes(ang, 1, 2)
    inv_re = jnp.pad(jnp.cos(angt) * weight, ((0, 0), (0, 0), (0, khalf - k1p)))
    inv_im = jnp.pad(-jnp.sin(angt) * weight, ((0, 0), (0, 0), (0, khalf - k1p)))
    inv = jnp.concatenate([inv_re, inv_im], axis=2).astype(BF)

    idx = jnp.arange(n2, dtype=jnp.int32)
    a2 = ((idx[:, None] * idx[None, :]) % n2).astype(F32) * (two_pi / n2)
    cr, ci = jnp.cos(a2), -jnp.sin(a2)
    mid_fwd = jnp.block([[cr, -ci], [ci, cr]]).astype(BF)
    mid_inv = jnp.block([[cr, ci], [-ci, cr]]).astype(BF)
    return fwd, inv, mid_fwd, mid_inv


def _dft_a_kernel(tab_ref, z_ref, f_ref, b_ref, az_ref, af_ref, ab_ref, pad_ref, *, group, d, h1, k1p):
    @pl.when(pl.program_id(0) == 0)
    def _():
        pad_ref[...] = jnp.zeros_like(pad_ref)

    for s, (src, dst) in enumerate(((z_ref, az_ref), (f_ref, af_ref), (b_ref, ab_ref))):
        for q in range(group):
            cols = pl.ds(q * d, d)
            pad_ref[s, pl.ds(0, h1), :] = src[:, cols]
            a = _dot(tab_ref[q], pad_ref[s])
            dst[0, :, cols] = a[:k1p].astype(BF)
            dst[1, :, cols] = a[k1p:].astype(BF)


def _dft_stage_a(tab, z, hf, hb, seq, d, *, group):
    n, n1, n2, h1, k1p = _dft_plan(seq)
    kin = tab.shape[2]
    view = lambda a: a.reshape(h1, n2 * d)
    src = pl.BlockSpec((h1, group * d), lambda s: (0, s))
    dst = pl.BlockSpec((2, k1p, group * d), lambda s: (0, 0, s))
    shape = jax.ShapeDtypeStruct((2, k1p, n2 * d), BF)
    return pl.pallas_call(
        functools.partial(_dft_a_kernel, group=group, d=d, h1=h1, k1p=k1p),
        grid=(n2 // group,),
        in_specs=[pl.BlockSpec((group, 2 * k1p, kin), lambda s: (s, 0, 0)), src, src, src],
        out_specs=[dst] * 3,
        out_shape=[shape] * 3,
        scratch_shapes=[pltpu.VMEM((3, kin, d), BF)],
        compiler_params=_params(48),
        name="dft_stage_a",
    )(tab, view(z), view(hf), view(hb))


def _dft_b_kernel(mf_ref, mi_ref, az_ref, af_ref, ab_ref, s_ref, *, n2, d, cw):
    mf = mf_ref[...]
    mi = mi_ref[...]
    for c in range(d // cw):
        cols = pl.ds(c * cw, cw)
        xz = _dot(mf, az_ref[:, 0, :, cols].reshape(2 * n2, cw))
        xf = _dot(mf, af_ref[:, 0, :, cols].reshape(2 * n2, cw))
        xb = _dot(mf, ab_ref[:, 0, :, cols].reshape(2 * n2, cw))
        kr = xf[:n2] + xb[:n2]
        ki = xf[n2:] - xb[n2:]
        yr = xz[:n2] * kr - xz[n2:] * ki
        yi = xz[:n2] * ki + xz[n2:] * kr
        s = _dot(mi, jnp.concatenate([yr, yi], axis=0).astype(BF))
        s_ref[0, 0, :, cols] = s[:n2].astype(BF)
        s_ref[1, 0, :, cols] = s[n2:].astype(BF)


def _dft_stage_b(mid_fwd, mid_inv, az, af, ab, seq, d, *, cw):
    n, n1, n2, h1, k1p = _dft_plan(seq)
    view = lambda a: a.reshape(2, k1p, n2, d)
    blk = pl.BlockSpec((2, 1, n2, d), lambda k: (0, k, 0, 0))
    mat = _resident((2 * n2, 2 * n2), lambda k: (0, 0))
    return pl.pallas_call(
        functools.partial(_dft_b_kernel, n2=n2, d=d, cw=cw),
        grid=(k1p,),
        in_specs=[mat, mat, blk, blk, blk],
        out_specs=blk,
        out_shape=jax.ShapeDtypeStruct((2, k1p, n2, d), BF),
        compiler_params=_params(48),
        name="dft_stage_b",
    )(mid_fwd, mid_inv, view(az), view(af), view(ab))


def _dft_c_kernel(tab_ref, s_ref, y_ref, pad_ref, *, group, d, k1p):
    khalf = pad_ref.shape[0] // 2

    @pl.when(pl.program_id(0) == 0)
    def _():
        pad_ref[...] = jnp.zeros_like(pad_ref)

    for q in range(group):
        cols = pl.ds(q * d, d)
        pad_ref[pl.ds(0, k1p), :] = s_ref[0, :, cols]
        pad_ref[pl.ds(khalf, k1p), :] = s_ref[1, :, cols]
        y_ref[:, cols] = _dot(tab_ref[q], pad_ref[...])


def _dft_stage_c(tab, s, seq, d, *, group):
    n, n1, n2, h1, k1p = _dft_plan(seq)
    kc = tab.shape[2]
    y = pl.pallas_call(
        functools.partial(_dft_c_kernel, group=group, d=d, k1p=k1p),
        grid=(n2 // group,),
        in_specs=[pl.BlockSpec((group, h1, kc), lambda q: (q, 0, 0)),
                  pl.BlockSpec((2, k1p, group * d), lambda q: (0, 0, q))],
        out_specs=pl.BlockSpec((h1, group * d), lambda q: (0, q)),
        out_shape=jax.ShapeDtypeStruct((h1, n2 * d), F32),
        scratch_shapes=[pltpu.VMEM((kc, d), BF)],
        compiler_params=_params(48),
        name="dft_stage_c",
    )(tab, s.reshape(2, k1p, n2 * d))
    return y.reshape(seq, d)


def _hy_out_kernel(x_ref, y_ref, z_ref, x0_ref, ssq_ref, skip_ref, wo_ref, o_ref):
    y = y_ref[...] * lax.rsqrt(ssq_ref[...]) + z_ref[...].astype(F32) * skip_ref[...]
    gated = (y * x0_ref[...].astype(F32)).astype(BF)
    o_ref[...] = x_ref[...] + _dot(gated, wo_ref[...])


def _hyena_out(x, y, z, x0, ssq, skip, w_out, j, *, tm):
    seq, d = x.shape
    tile = pl.BlockSpec((tm, d), lambda i: (i, 0))
    return pl.pallas_call(
        _hy_out_kernel,
        grid=(seq // tm,),
        in_specs=[tile, tile, tile, tile,
                  pl.BlockSpec((1, d), lambda i: (0, 0)),
                  pl.BlockSpec((None, 1, d), lambda i: (j, 0, 0)),
                  _resident((None, d, d), lambda i: (j, 0, 0))],
        out_specs=tile,
        out_shape=jax.ShapeDtypeStruct((seq, d), F32),
        compiler_params=_params(40),
        name="hyena_out",
    )(x, y, z, x0, ssq, skip, w_out)


def _hyena_mixer(x, g, w_in, taps, bias, f_w1, f_b1, f_w2, f_b2, f_w3, f_b3, f_w4, freq, skip, w_out,
                 layer, j, *, tm, tn):
    seq, d = x.shape
    x0, z = _hyena_in(x, g, w_in, taps, bias, layer, j, tm=tm, tn=tn)

    emb, order = f_w1.shape[1], f_w1.shape[2]
    bands = (emb - 1) // 2
    wp = _round_up(order, V7X_LANES)
    ep = _round_up(emb, V7X_LANES)
    pad2 = lambda a, r, c: jnp.pad(a.astype(F32), ((0, r - a.shape[0]), (0, c - a.shape[1])))
    freqs = jnp.linspace(1e-4, bands - 1, bands, dtype=F32)
    fpad = pad2(jnp.concatenate([jnp.zeros((1,), F32), freqs, freqs])[None, :], 1, ep)
    deltas = jnp.abs(jnp.linspace(math.log(HY_TARGET) / HY_FAST_DECAY_PCT,
                                  math.log(HY_TARGET) / HY_SLOW_DECAY_PCT, d, dtype=F32))[None, :]
    hf, hb, ssq = _hyena_filter(
        seq, d, fpad,
        pad2(f_w1[j], ep, wp), pad2(f_b1[j][None, :], 1, wp),
        pad2(f_w2[j], wp, wp), pad2(f_b2[j][None, :], 1, wp),
        pad2(f_w3[j], wp, wp), pad2(f_b3[j][None, :], 1, wp),
        pad2(f_w4[j], wp, 2 * d), pad2(freq[j][None, :], 1, wp), deltas, bands, tr=min(256, seq))

    tab_a, tab_c, mid_fwd, mid_inv = _dft_tables(seq)
    az, af, ab = _dft_stage_a(tab_a, z, hf, hb, seq, d, group=2)
    s = _dft_stage_b(mid_fwd, mid_inv, az, af, ab, seq, d, cw=min(512, d))
    y = _dft_stage_c(tab_c, s, seq, d, group=2)
    return _hyena_out(x, y, z, x0, ssq, skip, w_out, j, tm=min(256, seq))


def kernel(x, p, norm_ffn1, ffn1_w_gate, ffn1_w_up, ffn1_w_down, norm_mix, norm_ffn2, ffn2_w_gate, ffn2_w_up, ffn2_w_down, norm_ple, ple_w_gate, ple_w_up, sc_w_in, sc_conv, sc_w_out, cf_w_pw1, cf_b_pw1, cf_dw, cf_b_dw, cf_ln_g, cf_ln_b, cf_w_pw2, cf_b_pw2, hy_w_in, hy_short, hy_b_short, hy_f_w1, hy_f_b1, hy_f_w2, hy_f_b2, hy_f_w3, hy_f_b3, hy_f_w4, hy_freq, hy_skip, hy_w_out, norm_final):
    batch, seq, d = x.shape
    depth = p.shape[0]
    assert batch == 1, "the row tiling treats the sequence as the only token axis"
    tm = min(512, seq)
    tn = min(512, d)
    tf = min(512, ffn1_w_gate.shape[-1])

    bf = lambda a: a.astype(BF)
    row = lambda a: a[:, None, :].astype(F32)
    w1g, w1u, w1d = bf(ffn1_w_gate), bf(ffn1_w_up), bf(ffn1_w_down)
    w2g, w2u, w2d = bf(ffn2_w_gate), bf(ffn2_w_up), bf(ffn2_w_down)
    n_ffn1, n_mix, n_ffn2, n_ple = row(norm_ffn1), row(norm_mix), row(norm_ffn2), row(norm_ple)
    ple_gate, ple_up = bf(ple_w_gate), bf(ple_w_up)
    sc_in, sc_out = bf(sc_w_in), bf(sc_w_out)
    cf_w1, cf_w2 = bf(cf_w_pw1), bf(cf_w_pw2)
    hy_in, hy_out = bf(hy_w_in), bf(hy_w_out)
    gf = norm_final[None, :].astype(F32)

    xs = x.reshape(seq, d)
    ps = p.reshape(depth, seq, p.shape[-1])
    for i in range(depth):
        m, j = i % 3, i // 3
        xs = _ffn(xs, n_ffn1, w1g, w1u, w1d, i, tm=tm, tf=tf)
        if m == 0:
            xs = _short_conv_mixer(xs, n_mix, sc_in, sc_conv.astype(F32), sc_out, i, j, tm=tm, tn=tn)
        elif m == 1:
            xs = _conformer_mixer(xs, n_mix, cf_w1, row(cf_b_pw1), cf_dw.astype(F32), row(cf_b_dw),
                                  row(cf_ln_g), row(cf_ln_b), cf_w2, row(cf_b_pw2), i, j, tm=tm, tn=tn)
        else:
            xs = _hyena_mixer(xs, n_mix, hy_in, hy_short.astype(F32), row(hy_b_short),
                              hy_f_w1, hy_f_b1, hy_f_w2, hy_f_b2, hy_f_w3, hy_f_b3, hy_f_w4,
                              hy_freq, row(hy_skip), hy_out, i, j, tm=tm, tn=tn)
        xs = _ffn(xs, n_ffn2, w2g, w2u, w2d, i, tm=tm, tf=tf)
        xs = _ple(xs, ps, n_ple, ple_gate, ple_up, gf, i, tm=min(256, seq), final=(i == depth - 1))
    return xs.reshape(batch, seq, d)
```

```python
import functools
import math

import jax
import jax.numpy as jnp
from jax import lax
from jax.experimental import pallas as pl
from jax.experimental.pallas import tpu as pltpu

BF = jnp.bfloat16
F32 = jnp.float32

RMS_EPS = 1e-6
LN_EPS = 1e-5
HY_FAST_DECAY_PCT = 0.3
HY_SLOW_DECAY_PCT = 1.5
HY_TARGET = 1e-2

V7X_LANES = 128
V7X_SUBLANES = 8
V7X_BF16_SUBLANES = 16
HALO = V7X_BF16_SUBLANES
MIB = 1024 * 1024


def _dot(a, b):
    return jnp.dot(a, b, preferred_element_type=F32)


def _dot_hi(a, b):
    return jnp.dot(a, b, preferred_element_type=F32, precision=lax.Precision.HIGHEST)


def _rms(x, g):
    return x * lax.rsqrt(jnp.mean(x * x, axis=-1, keepdims=True) + RMS_EPS) * g


def _round_up(n, m):
    return (n + m - 1) // m * m


def _params(vmem_mib):
    return pltpu.CompilerParams(vmem_limit_bytes=int(vmem_mib * MIB))


def _resident(shape, index_map):
    return pl.BlockSpec(shape, index_map, pipeline_mode=pl.Buffered(1))


def _ffn_kernel(x_ref, g_ref, wg_ref, wu_ref, wd_ref, *rest, cast_blocks):
    if cast_blocks is None:
        o_ref, h_ref = rest
    else:
        ng_ref, nu_ref, nd_ref, o_ref, og_ref, ou_ref, od_ref, h_ref = rest
        step = pl.program_id(0) * pl.num_programs(1) + pl.program_id(1)

        @pl.when(step < cast_blocks[0])
        def _():
            og_ref[...] = ng_ref[...].astype(BF)
            ou_ref[...] = nu_ref[...].astype(BF)

        @pl.when(step < cast_blocks[1])
        def _():
            od_ref[...] = nd_ref[...].astype(BF)

    @pl.when(pl.program_id(1) == 0)
    def _():
        x = x_ref[...]
        h_ref[...] = _rms(x, g_ref[...]).astype(BF)
        o_ref[...] = x

    h = h_ref[...]
    a = _dot(h, wg_ref[...])
    u = _dot(h, wu_ref[...])
    act = (a * jax.nn.sigmoid(a) * u * 0.5).astype(BF)
    o_ref[...] += _dot(act, wd_ref[...])


def _cast_rows(rows, steps):
    r = _round_up(-(-rows // steps), V7X_BF16_SUBLANES)
    while rows % r:
        r += V7X_BF16_SUBLANES
    return r


def _ffn(x, g, wg, wu, wd, layer, nxt, *, tm, tf):
    seq, d = x.shape
    f = wg.shape[-1]
    nj = f // tf
    grid = (seq // tm, nj)
    in_specs = [
        pl.BlockSpec((tm, d), lambda i, j: (i, 0)),
        pl.BlockSpec((None, 1, d), lambda i, j: (layer, 0, 0)),
        pl.BlockSpec((d, tf), lambda i, j: (0, j)),
        pl.BlockSpec((d, tf), lambda i, j: (0, j)),
        pl.BlockSpec((tf, d), lambda i, j: (j, 0)),
    ]
    out_specs = [pl.BlockSpec((tm, d), lambda i, j: (i, 0))]
    out_shape = [jax.ShapeDtypeStruct((seq, d), F32)]
    args = [x, g, wg, wu, wd]
    cast_blocks = None
    if nxt is not None:
        ng, nu, nd, nl = nxt
        steps = grid[0] * grid[1]
        rg, rd = _cast_rows(d, steps), _cast_rows(f, steps)
        cast_blocks = (d // rg, f // rd)
        blk = lambda n: (lambda i, j: (jnp.minimum(i * nj + j, n - 1), 0))
        blk3 = lambda n: (lambda i, j: (nl, jnp.minimum(i * nj + j, n - 1), 0))
        in_specs += [pl.BlockSpec((None, rg, f), blk3(cast_blocks[0])),
                     pl.BlockSpec((None, rg, f), blk3(cast_blocks[0])),
                     pl.BlockSpec((None, rd, d), blk3(cast_blocks[1]))]
        out_specs += [pl.BlockSpec((rg, f), blk(cast_blocks[0])),
                      pl.BlockSpec((rg, f), blk(cast_blocks[0])),
                      pl.BlockSpec((rd, d), blk(cast_blocks[1]))]
        out_shape += [jax.ShapeDtypeStruct((d, f), BF), jax.ShapeDtypeStruct((d, f), BF),
                      jax.ShapeDtypeStruct((f, d), BF)]
        args += [ng, nu, nd]
    outs = pl.pallas_call(
        functools.partial(_ffn_kernel, cast_blocks=cast_blocks),
        grid=grid,
        in_specs=in_specs,
        out_specs=out_specs,
        out_shape=out_shape,
        scratch_shapes=[pltpu.VMEM((tm, d), BF)],
        compiler_params=_params(48),
        name="ffn",
    )(*args)
    return outs[0], (tuple(outs[1:]) if nxt is not None else None)


def _ple_kernel(x_ref, p_ref, g_ref, wgate_ref, wup_ref, gf_ref, o_ref, *, final):
    x = x_ref[...]
    h = _rms(x, g_ref[...]).astype(BF)
    gate = jax.nn.sigmoid(_dot(h, wgate_ref[...]))
    up = _dot(p_ref[...].astype(BF), wup_ref[...])
    y = x + up * gate
    if final:
        y = _rms(y, gf_ref[...])
    o_ref[...] = y


def _ple(x, p, g, wgate, wup, gf, layer, *, tm, final):
    seq, d = x.shape
    pd = p.shape[-1]
    return pl.pallas_call(
        functools.partial(_ple_kernel, final=final),
        grid=(seq // tm,),
        in_specs=[
            pl.BlockSpec((tm, d), lambda i: (i, 0)),
            pl.BlockSpec((None, tm, pd), lambda i: (layer, i, 0)),
            pl.BlockSpec((None, 1, d), lambda i: (layer, 0, 0)),
            _resident((None, d, d), lambda i: (layer, 0, 0)),
            _resident((None, pd, d), lambda i: (layer, 0, 0)),
            pl.BlockSpec((1, d), lambda i: (0, 0)),
        ],
        out_specs=pl.BlockSpec((tm, d), lambda i: (i, 0)),
        out_shape=jax.ShapeDtypeStruct((seq, d), F32),
        compiler_params=_params(40),
        name="ple",
    )(x, p, g, wgate, wup, gf)


def _halo_specs(tm, d, seq):
    per = tm // HALO
    last = seq // HALO - 1
    return [
        pl.BlockSpec((HALO, d), lambda i, j: (jnp.maximum(i * per - 1, 0), 0)),
        pl.BlockSpec((tm, d), lambda i, j: (i, 0)),
        pl.BlockSpec((HALO, d), lambda i, j: (jnp.minimum((i + 1) * per, last), 0)),
    ]


def _fill_normed(h_ref, xp_ref, x_ref, xn_ref, g, tm):
    h_ref[pl.ds(0, HALO), :] = _rms(xp_ref[...], g).astype(BF)
    h_ref[pl.ds(HALO, tm), :] = _rms(x_ref[...], g).astype(BF)
    h_ref[pl.ds(HALO + tm, HALO), :] = _rms(xn_ref[...], g).astype(BF)


def _in_sequence(tm, seq):
    row = pl.program_id(0) * tm - HALO + lax.broadcasted_iota(jnp.int32, (tm + 2 * HALO, 1), 0)
    return jnp.logical_and(row >= 0, row < seq)


def _dwconv(src_ref, taps, tm):
    k = taps.shape[0]
    acc = taps[0:1] * src_ref[pl.ds(HALO - k // 2, tm), :]
    for t in range(1, k):
        acc = acc + taps[t:t + 1] * src_ref[pl.ds(HALO - k // 2 + t, tm), :]
    return acc


def _dwconv_wide(src_ref, shift_ref, taps, tm):
    k = taps.shape[0]
    first = HALO - k // 2
    span = shift_ref.shape[1]
    for r in range(1, V7X_SUBLANES):
        shift_ref[r - 1] = src_ref[pl.ds(r, span), :]
    acc = None
    for t in range(k):
        base, r = divmod(first + t, V7X_SUBLANES)
        rows = pl.ds(base * V7X_SUBLANES, tm)
        term = taps[t:t + 1] * (src_ref[rows, :] if r == 0 else shift_ref[r - 1, rows, :])
        acc = term if acc is None else acc + term
    return acc


def _sc_kernel(xp_ref, x_ref, xn_ref, g_ref, wb_ref, wc_ref, wv_ref, taps_ref, wo_ref,
               o_ref, h_ref, cv_ref, *, tm, seq):
    @pl.when(pl.program_id(1) == 0)
    def _():
        _fill_normed(h_ref, xp_ref, x_ref, xn_ref, g_ref[...], tm)
        o_ref[...] = x_ref[...]

    h = h_ref[...]
    cv = _dot(h, wc_ref[...]) * _dot(h, wv_ref[...])
    cv_ref[...] = jnp.where(_in_sequence(tm, seq), cv, 0.0)
    conv = _dwconv(cv_ref, taps_ref[...], tm)
    b = _dot(h_ref[pl.ds(HALO, tm), :], wb_ref[...])
    o_ref[...] += _dot((b * conv).astype(BF), wo_ref[...])


def _short_conv_mixer(x, g, w_in, taps, w_out, layer, j, *, tm, tn):
    seq, d = x.shape
    nd = d // tn
    kw = taps.shape[1]
    return pl.pallas_call(
        functools.partial(_sc_kernel, tm=tm, seq=seq),
        grid=(seq // tm, nd),
        in_specs=_halo_specs(tm, d, seq) + [
            pl.BlockSpec((None, 1, d), lambda i, c: (layer, 0, 0)),
            pl.BlockSpec((None, d, tn), lambda i, c: (j, 0, c)),
            pl.BlockSpec((None, d, tn), lambda i, c: (j, 0, nd + c)),
            pl.BlockSpec((None, d, tn), lambda i, c: (j, 0, 2 * nd + c)),
            pl.BlockSpec((None, kw, tn), lambda i, c: (j, 0, c)),
            pl.BlockSpec((None, tn, d), lambda i, c: (j, c, 0)),
        ],
        out_specs=pl.BlockSpec((tm, d), lambda i, c: (i, 0)),
        out_shape=jax.ShapeDtypeStruct((seq, d), F32),
        scratch_shapes=[pltpu.VMEM((tm + 2 * HALO, d), BF),
                        pltpu.VMEM((tm + 2 * HALO, tn), F32)],
        compiler_params=_params(48),
        name="short_conv",
    )(x, x, x, g, w_in, w_in, w_in, taps, w_out)


def _cf_kernel(xp_ref, x_ref, xn_ref, g_ref, wa_ref, wg_ref, ba_ref, bg_ref, taps_ref, bdw_ref,
               lng_ref, lnb_ref, w2_ref, b2_ref, o_ref, h_ref, a_ref, shift_ref, conv_ref, *, tm, tn, seq):
    c = pl.program_id(1)
    nd = pl.num_programs(1)

    @pl.when(c == 0)
    def _():
        _fill_normed(h_ref, xp_ref, x_ref, xn_ref, g_ref[...], tm)

    h = h_ref[...]
    a = (_dot(h, wa_ref[...]) + ba_ref[...]) * jax.nn.sigmoid(_dot(h, wg_ref[...]) + bg_ref[...])
    a_ref[...] = jnp.where(_in_sequence(tm, seq), a, 0.0)
    conv_ref[c] = _dwconv_wide(a_ref, shift_ref, taps_ref[...], tm) + bdw_ref[...]

    @pl.when(c == nd - 1)
    def _():
        chunks = conv_ref.shape[0]
        d = chunks * tn
        total = jnp.zeros((tm, 1), F32)
        for k in range(chunks):
            total = total + jnp.sum(conv_ref[k], axis=-1, keepdims=True)
        mu = total / d
        sq = jnp.zeros((tm, 1), F32)
        for k in range(chunks):
            xc = conv_ref[k] - mu
            sq = sq + jnp.sum(xc * xc, axis=-1, keepdims=True)
        rstd = lax.rsqrt(sq / d + LN_EPS)
        o_ref[...] = x_ref[...] + b2_ref[...]
        for k in range(chunks):
            y = (conv_ref[k] - mu) * rstd * lng_ref[:, k * tn:(k + 1) * tn] + lnb_ref[:, k * tn:(k + 1) * tn]
            s = (y * jax.nn.sigmoid(y)).astype(BF)
            o_ref[...] += _dot(s, w2_ref[k * tn:(k + 1) * tn, :])


def _conformer_mixer(x, g, w1, b1, taps, bdw, lng, lnb, w2, b2, layer, j, *, tm, tn):
    seq, d = x.shape
    nd = d // tn
    kw = taps.shape[1]
    span = (HALO + kw // 2) // V7X_SUBLANES * V7X_SUBLANES + tm
    row = lambda i, c: (j, 0, 0)
    return pl.pallas_call(
        functools.partial(_cf_kernel, tm=tm, tn=tn, seq=seq),
        grid=(seq // tm, nd),
        in_specs=_halo_specs(tm, d, seq) + [
            pl.BlockSpec((None, 1, d), lambda i, c: (layer, 0, 0)),
            pl.BlockSpec((None, d, tn), lambda i, c: (j, 0, c)),
            pl.BlockSpec((None, d, tn), lambda i, c: (j, 0, nd + c)),
            pl.BlockSpec((None, 1, tn), lambda i, c: (j, 0, c)),
            pl.BlockSpec((None, 1, tn), lambda i, c: (j, 0, nd + c)),
            pl.BlockSpec((None, kw, tn), lambda i, c: (j, 0, c)),
            pl.BlockSpec((None, 1, tn), lambda i, c: (j, 0, c)),
            pl.BlockSpec((None, 1, d), row),
            pl.BlockSpec((None, 1, d), row),
            _resident((None, d, d), row),
            pl.BlockSpec((None, 1, d), row),
        ],
        out_specs=pl.BlockSpec((tm, d), lambda i, c: (i, 0)),
        out_shape=jax.ShapeDtypeStruct((seq, d), F32),
        scratch_shapes=[pltpu.VMEM((tm + 2 * HALO, d), BF),
                        pltpu.VMEM((tm + 2 * HALO, tn), F32),
                        pltpu.VMEM((V7X_SUBLANES - 1, span, tn), F32),
                        pltpu.VMEM((nd, tm, tn), F32)],
        compiler_params=_params(48),
        name="conformer",
    )(x, x, x, g, w1, w1, b1, b1, taps, bdw, lng, lnb, w2, b2)


def _hy_in_kernel(xp_ref, x_ref, xn_ref, g_ref, w0_ref, w1_ref, w2_ref, t0_ref, t1_ref, t2_ref,
                  b0_ref, b1_ref, b2_ref, x0_ref, z_ref, h_ref, u_ref, *, tm, seq):
    @pl.when(pl.program_id(1) == 0)
    def _():
        _fill_normed(h_ref, xp_ref, x_ref, xn_ref, g_ref[...], tm)

    h = h_ref[...]
    inside = _in_sequence(tm, seq)

    def branch(w_ref, t_ref, b_ref):
        u_ref[...] = jnp.where(inside, _dot(h, w_ref[...]), 0.0)
        return _dwconv(u_ref, t_ref[...], tm) + b_ref[...]

    x0_ref[...] = branch(w0_ref, t0_ref, b0_ref).astype(BF)
    x1 = branch(w1_ref, t1_ref, b1_ref)
    z_ref[...] = (branch(w2_ref, t2_ref, b2_ref) * x1).astype(BF)


def _hyena_in(x, g, w_in, taps, bias, layer, j, *, tm, tn):
    seq, d = x.shape
    nd = d // tn
    kw = taps.shape[1]
    col = lambda k: (lambda i, c: (j, 0, k * nd + c))
    return pl.pallas_call(
        functools.partial(_hy_in_kernel, tm=tm, seq=seq),
        grid=(seq // tm, nd),
        in_specs=_halo_specs(tm, d, seq)
        + [pl.BlockSpec((None, 1, d), lambda i, c: (layer, 0, 0))]
        + [pl.BlockSpec((None, d, tn), col(k)) for k in range(3)]
        + [pl.BlockSpec((None, kw, tn), col(k)) for k in range(3)]
        + [pl.BlockSpec((None, 1, tn), col(k)) for k in range(3)],
        out_specs=[pl.BlockSpec((tm, tn), lambda i, c: (i, c))] * 2,
        out_shape=[jax.ShapeDtypeStruct((seq, d), BF)] * 2,
        scratch_shapes=[pltpu.VMEM((tm + 2 * HALO, d), BF),
                        pltpu.VMEM((tm + 2 * HALO, tn), F32)],
        compiler_params=_params(40),
        name="hyena_in",
    )(x, x, x, g, w_in, w_in, w_in, taps, taps, taps, bias, bias, bias)


GROUP = V7X_BF16_SUBLANES


def _dft_plan(seq):
    n = 2 * seq
    n2 = 256 if n >= 256 * 32 else 64
    n1 = n // n2
    h1 = n1 // 2
    assert h1 & (h1 - 1) == 0 and h1 >= V7X_BF16_SUBLANES, "sequence length must be a power of two"
    k1p = _round_up(h1 + 1, V7X_BF16_SUBLANES)
    pack = max(1, V7X_LANES // h1)
    return n, n1, n2, h1, k1p, pack


def _dft_tables(seq):
    n, n1, n2, h1, k1p, pack = _dft_plan(seq)
    khalf = _round_up(k1p, V7X_LANES)
    two_pi = 2.0 * math.pi

    k1 = jnp.arange(k1p, dtype=jnp.int32)
    pos = (jnp.arange(h1, dtype=jnp.int32)[None, :] * n2 + jnp.arange(n2, dtype=jnp.int32)[:, None])
    phase = (k1[None, :, None] * pos[:, None, :]) % n
    ang = phase.astype(F32) * (two_pi / n)
    fwd = jnp.concatenate([jnp.cos(ang), -jnp.sin(ang)], axis=1)
    fwd = fwd.reshape(n2 // pack, pack, 2 * k1p, 1, h1) * jnp.eye(pack, dtype=F32)[None, :, None, :, None]
    fwd = fwd.reshape(n2 // pack, pack * 2 * k1p, pack * h1).astype(BF)

    weight = jnp.where(k1 == 0, 1.0, jnp.where(k1 < n1 // 2, 2.0, jnp.where(k1 == n1 // 2, 1.0, 0.0))) / n
    angt = jnp.swapaxes(ang, 1, 2)
    inv_re = jnp.pad(jnp.cos(angt) * weight, ((0, 0), (0, 0), (0, khalf - k1p)))
    inv_im = jnp.pad(-jnp.sin(angt) * weight, ((0, 0), (0, 0), (0, khalf - k1p)))
    inv = jnp.concatenate([inv_re, inv_im], axis=2).astype(BF)

    idx = jnp.arange(n2, dtype=jnp.int32)
    a2 = ((idx[:, None] * idx[None, :]) % n2).astype(F32) * (two_pi / n2)
    cr, ci = jnp.cos(a2), -jnp.sin(a2)
    mid_fwd = jnp.block([[cr, -ci], [ci, cr]]).astype(BF)
    mid_inv = jnp.block([[cr, ci], [-ci, cr]]).astype(BF)
    return fwd, inv, mid_fwd, mid_inv


def _group_positions(step, h1, n2, first=0, rows=None):
    rows = GROUP * h1 if rows is None else rows
    e = first + lax.broadcasted_iota(jnp.int32, (rows, 1), 0)
    q = lax.shift_right_logical(e, jnp.int32(int(math.log2(h1))))
    return (e - q * h1) * n2 + step * GROUP + q


def _hy_mlp_kernel(fpad_ref, w1_ref, b1_ref, w2_ref, b2_ref, w3_ref, b3_ref, fr_ref, a_ref,
                   *, h1, n2, seq, bands):
    row = _group_positions(pl.program_id(0), h1, n2).astype(F32)
    t = row / (seq - 1)
    arg = ((2.0 * math.pi / seq) * row) * fpad_ref[...]
    lane = lax.broadcasted_iota(jnp.int32, (1, fpad_ref.shape[1]), 1)
    feat = jnp.where(lane == 0, t,
                     jnp.where(lane <= bands, jnp.cos(arg),
                               jnp.where(lane <= 2 * bands, -jnp.sin(arg), 0.0)))
    fr = fr_ref[...]
    a = jnp.sin(fr * (_dot_hi(feat, w1_ref[...]) + b1_ref[...]))
    a = jnp.sin(fr * (_dot_hi(a, w2_ref[...]) + b2_ref[...]))
    a = jnp.sin(fr * (_dot_hi(a, w3_ref[...]) + b3_ref[...]))
    a_ref[...] = a.astype(BF)


def _hyena_filter_features(seq, fpad, w1, b1, w2, b2, w3, b3, fr, bands):
    n, n1, n2, h1, k1p, pack = _dft_plan(seq)
    wp = w1.shape[1]
    full = lambda a: pl.BlockSpec(a.shape, lambda s: (0,) * a.ndim)
    args = (fpad, w1, b1, w2, b2, w3, b3, fr)
    return pl.pallas_call(
        functools.partial(_hy_mlp_kernel, h1=h1, n2=n2, seq=seq, bands=bands),
        grid=(n2 // GROUP,),
        in_specs=[full(a) for a in args],
        out_specs=pl.BlockSpec((GROUP * h1, wp), lambda s: (s, 0)),
        out_shape=jax.ShapeDtypeStruct((seq, wp), BF),
        compiler_params=_params(32),
        name="hyena_filter_mlp",
    )(*args)


def _dft_a_kernel(tab_ref, z_ref, a3_ref, w4f_ref, w4b_ref, dl_ref, az_ref, af_ref, ab_ref, ssq_ref,
                  zt_ref, stage_ref, *, h1, n2, k1p, pack, seq):
    step = pl.program_id(1)
    cw = z_ref.shape[-1]
    kin = pack * h1

    @pl.when(step == 0)
    def _():
        ssq_ref[...] = jnp.zeros_like(ssq_ref)

    zt_ref[...] = pltpu.einshape("abc->bac", z_ref[...]).reshape(GROUP * h1, cw)
    ssq = jnp.zeros((1, cw), F32)
    for p in range(GROUP // pack):
        rows = pl.ds(p * kin, kin)
        pos = _group_positions(step, h1, n2, p * kin, kin)
        decay = jnp.exp(-(pos.astype(F32) / (seq - 1)) * dl_ref[...])
        a3 = a3_ref[rows, :]
        hf = _dot(a3, w4f_ref[...]) * decay
        hb = jnp.where(pos == 0, 0.0, _dot(a3, w4b_ref[...]) * decay)
        ssq = ssq + jnp.sum(hf * hf + hb * hb, axis=0, keepdims=True)
        tab = tab_ref[p]
        for s, src in enumerate((zt_ref[rows, :], hf.astype(BF), hb.astype(BF))):
            a = _dot(tab, src)
            stage_ref[s, pl.ds(p * pack, pack)] = a.astype(BF).reshape(pack, 2 * k1p, cw)
    ssq_ref[...] += ssq
    for s, dst in enumerate((az_ref, af_ref, ab_ref)):
        dst[...] = pltpu.einshape("qkc->kqc", stage_ref[s])


def _dft_stage_a(tab, z, a3, w4, deltas, seq, d, *, cw):
    n, n1, n2, h1, k1p, pack = _dft_plan(seq)
    wp = a3.shape[1]
    nc = d // cw
    spectrum = pl.BlockSpec((2 * k1p, GROUP, cw), lambda c, s: (0, s, c))
    shape = jax.ShapeDtypeStruct((2 * k1p, n2, d), BF)
    return pl.pallas_call(
        functools.partial(_dft_a_kernel, h1=h1, n2=n2, k1p=k1p, pack=pack, seq=seq),
        grid=(nc, n2 // GROUP),
        in_specs=[pl.BlockSpec((GROUP // pack,) + tab.shape[1:], lambda c, s: (s, 0, 0)),
                  pl.BlockSpec((h1, GROUP, cw), lambda c, s: (0, s, c)),
                  pl.BlockSpec((GROUP * h1, wp), lambda c, s: (s, 0)),
                  pl.BlockSpec((wp, cw), lambda c, s: (0, c)),
                  pl.BlockSpec((wp, cw), lambda c, s: (0, nc + c)),
                  pl.BlockSpec((1, cw), lambda c, s: (0, c))],
        out_specs=[spectrum] * 3 + [pl.BlockSpec((1, cw), lambda c, s: (0, c))],
        out_shape=[shape] * 3 + [jax.ShapeDtypeStruct((1, d), F32)],
        scratch_shapes=[pltpu.VMEM((GROUP * h1, cw), BF),
                        pltpu.VMEM((3, GROUP, 2 * k1p, cw), BF)],
        compiler_params=_params(48),
        name="dft_stage_a",
    )(tab, z.reshape(h1, n2, d), a3, w4, w4, deltas)


def _dft_b_kernel(mf_ref, mi_ref, az_ref, af_ref, ab_ref, s_ref, *, n2, d, cw):
    mf = mf_ref[...]
    mi = mi_ref[...]
    for c in range(d // cw):
        cols = pl.ds(c * cw, cw)
        xz = _dot(mf, az_ref[:, 0, :, cols].reshape(2 * n2, cw))
        xf = _dot(mf, af_ref[:, 0, :, cols].reshape(2 * n2, cw))
        xb = _dot(mf, ab_ref[:, 0, :, cols].reshape(2 * n2, cw))
        kr = xf[:n2] + xb[:n2]
        ki = xf[n2:] - xb[n2:]
        yr = xz[:n2] * kr - xz[n2:] * ki
        yi = xz[:n2] * ki + xz[n2:] * kr
        s = _dot(mi, jnp.concatenate([yr, yi], axis=0).astype(BF))
        s_ref[0, 0, :, cols] = s[:n2].astype(BF)
        s_ref[1, 0, :, cols] = s[n2:].astype(BF)


def _dft_stage_b(mid_fwd, mid_inv, az, af, ab, seq, d, *, cw):
    n, n1, n2, h1, k1p, pack = _dft_plan(seq)
    view = lambda a: a.reshape(2, k1p, n2, d)
    blk = pl.BlockSpec((2, 1, n2, d), lambda k: (0, k, 0, 0))
    mat = _resident((2 * n2, 2 * n2), lambda k: (0, 0))
    return pl.pallas_call(
        functools.partial(_dft_b_kernel, n2=n2, d=d, cw=cw),
        grid=(k1p,),
        in_specs=[mat, mat, blk, blk, blk],
        out_specs=blk,
        out_shape=jax.ShapeDtypeStruct((2, k1p, n2, d), BF),
        compiler_params=_params(48),
        name="dft_stage_b",
    )(mid_fwd, mid_inv, view(az), view(af), view(ab))


def _dft_c_kernel(tab_ref, s_ref, y_ref, pad_ref, st_ref, ys_ref, *, k1p):
    khalf = pad_ref.shape[0] // 2

    @pl.when(jnp.logical_and(pl.program_id(0) == 0, pl.program_id(1) == 0))
    def _():
        pad_ref[...] = jnp.zeros_like(pad_ref)

    st_ref[...] = pltpu.einshape("kqc->qkc", s_ref[...])
    for q in range(GROUP):
        pad_ref[pl.ds(0, k1p), :] = st_ref[q, pl.ds(0, k1p), :]
        pad_ref[pl.ds(khalf, k1p), :] = st_ref[q, pl.ds(k1p, k1p), :]
        ys_ref[q] = _dot(tab_ref[q], pad_ref[...]).astype(BF)
    y_ref[...] = pltpu.einshape("qnc->nqc", ys_ref[...])


def _dft_stage_c(tab, s, seq, d, *, cw):
    n, n1, n2, h1, k1p, pack = _dft_plan(seq)
    kc = tab.shape[2]
    y = pl.pallas_call(
        functools.partial(_dft_c_kernel, k1p=k1p),
        grid=(d // cw, n2 // GROUP),
        in_specs=[pl.BlockSpec((GROUP, h1, kc), lambda c, q: (q, 0, 0)),
                  pl.BlockSpec((2 * k1p, GROUP, cw), lambda c, q: (0, q, c))],
        out_specs=pl.BlockSpec((h1, GROUP, cw), lambda c, q: (0, q, c)),
        out_shape=jax.ShapeDtypeStruct((h1, n2, d), BF),
        scratch_shapes=[pltpu.VMEM((kc, cw), BF),
                        pltpu.VMEM((GROUP, 2 * k1p, cw), BF),
                        pltpu.VMEM((GROUP, h1, cw), BF)],
        compiler_params=_params(40),
        name="dft_stage_c",
    )(tab, s.reshape(2 * k1p, n2, d))
    return y.reshape(seq, d)


def _hy_out_kernel(x_ref, y_ref, z_ref, x0_ref, ssq_ref, skip_ref, wo_ref, o_ref):
    y = y_ref[...].astype(F32) * lax.rsqrt(ssq_ref[...]) + z_ref[...].astype(F32) * skip_ref[...]
    gated = (y * x0_ref[...].astype(F32)).astype(BF)
    o_ref[...] = x_ref[...] + _dot(gated, wo_ref[...])


def _hyena_out(x, y, z, x0, ssq, skip, w_out, j, *, tm):
    seq, d = x.shape
    tile = pl.BlockSpec((tm, d), lambda i: (i, 0))
    return pl.pallas_call(
        _hy_out_kernel,
        grid=(seq // tm,),
        in_specs=[tile, tile, tile, tile,
                  pl.BlockSpec((1, d), lambda i: (0, 0)),
                  pl.BlockSpec((None, 1, d), lambda i: (j, 0, 0)),
                  _resident((None, d, d), lambda i: (j, 0, 0))],
        out_specs=tile,
        out_shape=jax.ShapeDtypeStruct((seq, d), F32),
        compiler_params=_params(40),
        name="hyena_out",
    )(x, y, z, x0, ssq, skip, w_out)


def _hyena_mixer(x, g, w_in, taps, bias, f_w1, f_b1, f_w2, f_b2, f_w3, f_b3, f_w4, freq, skip, w_out,
                 layer, j, *, tm, tn):
    seq, d = x.shape
    x0, z = _hyena_in(x, g, w_in, taps, bias, layer, j, tm=tm, tn=tn)

    emb, order = f_w1.shape[1], f_w1.shape[2]
    bands = (emb - 1) // 2
    wp = _round_up(order, V7X_LANES)
    ep = _round_up(emb, V7X_LANES)
    pad2 = lambda a, r, c: jnp.pad(a.astype(F32), ((0, r - a.shape[0]), (0, c - a.shape[1])))
    freqs = jnp.linspace(1e-4, bands - 1, bands, dtype=F32)
    fpad = pad2(jnp.concatenate([jnp.zeros((1,), F32), freqs, freqs])[None, :], 1, ep)
    deltas = jnp.abs(jnp.linspace(math.log(HY_TARGET) / HY_FAST_DECAY_PCT,
                                  math.log(HY_TARGET) / HY_SLOW_DECAY_PCT, d, dtype=F32))[None, :]
    a3 = _hyena_filter_features(
        seq, fpad,
        pad2(f_w1[j], ep, wp), pad2(f_b1[j][None, :], 1, wp),
        pad2(f_w2[j], wp, wp), pad2(f_b2[j][None, :], 1, wp),
        pad2(f_w3[j], wp, wp), pad2(f_b3[j][None, :], 1, wp),
        pad2(freq[j][None, :], 1, wp), bands)

    cw = min(512, d)
    tab_a, tab_c, mid_fwd, mid_inv = _dft_tables(seq)
    az, af, ab, ssq = _dft_stage_a(tab_a, z, a3, pad2(f_w4[j], wp, 2 * d).astype(BF), deltas, seq, d, cw=cw)
    s = _dft_stage_b(mid_fwd, mid_inv, az, af, ab, seq, d, cw=cw)
    y = _dft_stage_c(tab_c, s, seq, d, cw=cw)
    return _hyena_out(x, y, z, x0, ssq, skip, w_out, j, tm=min(256, seq))


def kernel(x, p, norm_ffn1, ffn1_w_gate, ffn1_w_up, ffn1_w_down, norm_mix, norm_ffn2, ffn2_w_gate, ffn2_w_up, ffn2_w_down, norm_ple, ple_w_gate, ple_w_up, sc_w_in, sc_conv, sc_w_out, cf_w_pw1, cf_b_pw1, cf_dw, cf_b_dw, cf_ln_g, cf_ln_b, cf_w_pw2, cf_b_pw2, hy_w_in, hy_short, hy_b_short, hy_f_w1, hy_f_b1, hy_f_w2, hy_f_b2, hy_f_w3, hy_f_b3, hy_f_w4, hy_freq, hy_skip, hy_w_out, norm_final):
    batch, seq, d = x.shape
    depth = p.shape[0]
    assert batch == 1, "the row tiling treats the sequence as the only token axis"
    tm = min(512, seq)
    tn = min(512, d)
    tn_cf = min(256, d)
    tf = min(512, ffn1_w_gate.shape[-1])

    bf = lambda a: a.astype(BF)
    row = lambda a: a[:, None, :].astype(F32)
    ffn1_w = (ffn1_w_gate, ffn1_w_up, ffn1_w_down)
    ffn2_w = (ffn2_w_gate, ffn2_w_up, ffn2_w_down)
    w_cur = tuple(bf(w[0]) for w in ffn1_w)
    n_ffn1, n_mix, n_ffn2, n_ple = row(norm_ffn1), row(norm_mix), row(norm_ffn2), row(norm_ple)
    ple_gate, ple_up = bf(ple_w_gate), bf(ple_w_up)
    sc_in, sc_out = bf(sc_w_in), bf(sc_w_out)
    cf_w1, cf_w2 = bf(cf_w_pw1), bf(cf_w_pw2)
    hy_in, hy_out = bf(hy_w_in), bf(hy_w_out)
    gf = norm_final[None, :].astype(F32)

    xs = x.reshape(seq, d)
    ps = p.reshape(depth, seq, p.shape[-1])
    for i in range(depth):
        m, j = i % 3, i // 3
        xs, w_nxt = _ffn(xs, n_ffn1, *w_cur, i, ffn2_w + (i,), tm=tm, tf=tf)
        if m == 0:
            xs = _short_conv_mixer(xs, n_mix, sc_in, sc_conv.astype(F32), sc_out, i, j, tm=tm, tn=tn)
        elif m == 1:
            xs = _conformer_mixer(xs, n_mix, cf_w1, row(cf_b_pw1), cf_dw.astype(F32), row(cf_b_dw),
                                  row(cf_ln_g), row(cf_ln_b), cf_w2, row(cf_b_pw2), i, j, tm=tm, tn=tn_cf)
        else:
            xs = _hyena_mixer(xs, n_mix, hy_in, hy_short.astype(F32), row(hy_b_short),
                              hy_f_w1, hy_f_b1, hy_f_w2, hy_f_b2, hy_f_w3, hy_f_b3, hy_f_w4,
                              hy_freq, row(hy_skip), hy_out, i, j, tm=tm, tn=tn)
        xs, w_cur = _ffn(xs, n_ffn2, *w_nxt, i, ffn1_w + (i + 1,) if i + 1 < depth else None, tm=tm, tf=tf)
        xs = _ple(xs, ps, n_ple, ple_gate, ple_up, gf, i, tm=min(256, seq), final=(i == depth - 1))
    return xs.reshape(batch, seq, d)
```

```python
import functools
import math

import jax
import jax.numpy as jnp
import numpy as np
from jax import lax
from jax.experimental import pallas as pl
from jax.experimental.pallas import tpu as pltpu

BF = jnp.bfloat16
F32 = jnp.float32

RMS_EPS = 1e-6
LN_EPS = 1e-5
HY_FAST_DECAY_PCT = 0.3
HY_SLOW_DECAY_PCT = 1.5
HY_TARGET = 1e-2

V7X_LANES = 128
V7X_SUBLANES = 8
V7X_BF16_SUBLANES = 16
HALO = V7X_BF16_SUBLANES
FFN_X_PARTS = 4
MIB = 1024 * 1024


def _dot(a, b):
    return jnp.dot(a, b, preferred_element_type=F32)


def _dot_hi(a, b):
    return jnp.dot(a, b, preferred_element_type=F32, precision=lax.Precision.HIGHEST)


def _rms(x, g):
    return x * lax.rsqrt(jnp.mean(x * x, axis=-1, keepdims=True) + RMS_EPS) * g


def _round_up(n, m):
    return (n + m - 1) // m * m


def _params(vmem_mib):
    return pltpu.CompilerParams(vmem_limit_bytes=int(vmem_mib * MIB))


def _resident(shape, index_map):
    return pl.BlockSpec(shape, index_map, pipeline_mode=pl.Buffered(1))


def _ffn_kernel(x_ref, g_ref, wg_ref, wu_ref, wd_ref, *rest, cast_blocks):
    if cast_blocks is None:
        o_ref, h_ref = rest
    else:
        ng_ref, nu_ref, nd_ref, o_ref, og_ref, ou_ref, od_ref, h_ref = rest
        step = pl.program_id(0) * pl.num_programs(1) + pl.program_id(1)

        @pl.when(step < cast_blocks[0])
        def _():
            og_ref[...] = ng_ref[...].astype(BF)
            ou_ref[...] = nu_ref[...].astype(BF)

        @pl.when(step < cast_blocks[1])
        def _():
            od_ref[...] = nd_ref[...].astype(BF)

    j = pl.program_id(1)
    rows = x_ref.shape[0]
    for part in range(FFN_X_PARTS):
        @pl.when(j == part)
        def _():
            x = x_ref[...]
            h_ref[pl.ds(part * rows, rows), :] = _rms(x, g_ref[...]).astype(BF)
            o_ref[pl.ds(part * rows, rows), :] = x

    @pl.when(j >= FFN_X_PARTS - 1)
    def _():
        h = h_ref[...]
        a = _dot(h, wg_ref[...])
        u = _dot(h, wu_ref[...])
        act = (a * jax.nn.sigmoid(a) * u * 0.5).astype(BF)
        o_ref[...] += _dot(act, wd_ref[...])


def _cast_rows(rows, steps):
    r = _round_up(-(-rows // steps), V7X_BF16_SUBLANES)
    while rows % r:
        r += V7X_BF16_SUBLANES
    return r


def _ffn(x, g, wg, wu, wd, layer, nxt, *, tm, tf):
    seq, d = x.shape
    f = wg.shape[-1]
    lead = FFN_X_PARTS - 1
    nj = f // tf + lead
    grid = (seq // tm, nj)
    chunk = lambda j: jnp.maximum(j - lead, 0)
    in_specs = [
        pl.BlockSpec((tm // FFN_X_PARTS, d), lambda i, j: (FFN_X_PARTS * i + jnp.minimum(j, lead), 0)),
        pl.BlockSpec((None, 1, d), lambda i, j: (layer, 0, 0)),
        pl.BlockSpec((d, tf), lambda i, j: (0, chunk(j))),
        pl.BlockSpec((d, tf), lambda i, j: (0, chunk(j))),
        pl.BlockSpec((tf, d), lambda i, j: (chunk(j), 0)),
    ]
    out_specs = [pl.BlockSpec((tm, d), lambda i, j: (i, 0))]
    out_shape = [jax.ShapeDtypeStruct((seq, d), F32)]
    args = [x, g, wg, wu, wd]
    cast_blocks = None
    if nxt is not None:
        ng, nu, nd, nl = nxt
        steps = grid[0] * grid[1]
        rg, rd = _cast_rows(d, steps), _cast_rows(f, steps)
        cast_blocks = (d // rg, f // rd)
        blk = lambda n: (lambda i, j: (jnp.minimum(i * nj + j, n - 1), 0))
        blk3 = lambda n: (lambda i, j: (nl, jnp.minimum(i * nj + j, n - 1), 0))
        in_specs += [pl.BlockSpec((None, rg, f), blk3(cast_blocks[0])),
                     pl.BlockSpec((None, rg, f), blk3(cast_blocks[0])),
                     pl.BlockSpec((None, rd, d), blk3(cast_blocks[1]))]
        out_specs += [pl.BlockSpec((rg, f), blk(cast_blocks[0])),
                      pl.BlockSpec((rg, f), blk(cast_blocks[0])),
                      pl.BlockSpec((rd, d), blk(cast_blocks[1]))]
        out_shape += [jax.ShapeDtypeStruct((d, f), BF), jax.ShapeDtypeStruct((d, f), BF),
                      jax.ShapeDtypeStruct((f, d), BF)]
        args += [ng, nu, nd]
    outs = pl.pallas_call(
        functools.partial(_ffn_kernel, cast_blocks=cast_blocks),
        grid=grid,
        in_specs=in_specs,
        out_specs=out_specs,
        out_shape=out_shape,
        scratch_shapes=[pltpu.VMEM((tm, d), BF)],
        compiler_params=_params(48),
        name="ffn",
    )(*args)
    return outs[0], (tuple(outs[1:]) if nxt is not None else None)


def _ple_kernel(x_ref, p_ref, g_ref, wgate_ref, wup_ref, gf_ref, o_ref, *, final):
    x = x_ref[...]
    h = _rms(x, g_ref[...]).astype(BF)
    gate = jax.nn.sigmoid(_dot(h, wgate_ref[...]))
    up = _dot(p_ref[...].astype(BF), wup_ref[...])
    y = x + up * gate
    if final:
        y = _rms(y, gf_ref[...])
    o_ref[...] = y


def _ple(x, p, g, wgate, wup, gf, layer, *, tm, final):
    seq, d = x.shape
    pd = p.shape[-1]
    return pl.pallas_call(
        functools.partial(_ple_kernel, final=final),
        grid=(seq // tm,),
        in_specs=[
            pl.BlockSpec((tm, d), lambda i: (i, 0)),
            pl.BlockSpec((None, tm, pd), lambda i: (layer, i, 0)),
            pl.BlockSpec((None, 1, d), lambda i: (layer, 0, 0)),
            _resident((None, d, d), lambda i: (layer, 0, 0)),
            _resident((None, pd, d), lambda i: (layer, 0, 0)),
            pl.BlockSpec((1, d), lambda i: (0, 0)),
        ],
        out_specs=pl.BlockSpec((tm, d), lambda i: (i, 0)),
        out_shape=jax.ShapeDtypeStruct((seq, d), F32),
        compiler_params=_params(48),
        name="ple",
    )(x, p, g, wgate, wup, gf)


def _halo_specs(tm, d, seq):
    per = tm // HALO
    last = seq // HALO - 1
    return [
        pl.BlockSpec((HALO, d), lambda i, j: (jnp.maximum(i * per - 1, 0), 0)),
        pl.BlockSpec((tm, d), lambda i, j: (i, 0)),
        pl.BlockSpec((HALO, d), lambda i, j: (jnp.minimum((i + 1) * per, last), 0)),
    ]


def _fill_normed(h_ref, xp_ref, x_ref, xn_ref, g, tm):
    h_ref[pl.ds(0, HALO), :] = _rms(xp_ref[...], g).astype(BF)
    h_ref[pl.ds(HALO, tm), :] = _rms(x_ref[...], g).astype(BF)
    h_ref[pl.ds(HALO + tm, HALO), :] = _rms(xn_ref[...], g).astype(BF)


def _in_sequence(tm, seq):
    row = pl.program_id(0) * tm - HALO + lax.broadcasted_iota(jnp.int32, (tm + 2 * HALO, 1), 0)
    return jnp.logical_and(row >= 0, row < seq)


def _dwconv(src_ref, taps, tm):
    k = taps.shape[0]
    acc = taps[0:1] * src_ref[pl.ds(HALO - k // 2, tm), :]
    for t in range(1, k):
        acc = acc + taps[t:t + 1] * src_ref[pl.ds(HALO - k // 2 + t, tm), :]
    return acc


def _dwconv_wide(src_ref, shift_ref, taps, tm):
    k = taps.shape[0]
    first = HALO - k // 2
    span = shift_ref.shape[1]
    for r in range(1, V7X_SUBLANES):
        shift_ref[r - 1] = src_ref[pl.ds(r, span), :]
    acc = None
    for t in range(k):
        base, r = divmod(first + t, V7X_SUBLANES)
        rows = pl.ds(base * V7X_SUBLANES, tm)
        term = taps[t:t + 1] * (src_ref[rows, :] if r == 0 else shift_ref[r - 1, rows, :])
        acc = term if acc is None else acc + term
    return acc


def _sc_kernel(xp_ref, x_ref, xn_ref, g_ref, wb_ref, wc_ref, wv_ref, taps_ref, wo_ref,
               o_ref, h_ref, cv_ref, *, tm, seq):
    @pl.when(pl.program_id(1) == 0)
    def _():
        _fill_normed(h_ref, xp_ref, x_ref, xn_ref, g_ref[...], tm)
        o_ref[...] = x_ref[...]

    h = h_ref[...]
    cv = _dot(h, wc_ref[...]) * _dot(h, wv_ref[...])
    cv_ref[...] = jnp.where(_in_sequence(tm, seq), cv, 0.0)
    conv = _dwconv(cv_ref, taps_ref[...], tm)
    b = _dot(h_ref[pl.ds(HALO, tm), :], wb_ref[...])
    o_ref[...] += _dot((b * conv).astype(BF), wo_ref[...])


def _short_conv_mixer(x, g, w_in, taps, w_out, layer, j, *, tm, tn):
    seq, d = x.shape
    nd = d // tn
    kw = taps.shape[1]
    return pl.pallas_call(
        functools.partial(_sc_kernel, tm=tm, seq=seq),
        grid=(seq // tm, nd),
        in_specs=_halo_specs(tm, d, seq) + [
            pl.BlockSpec((None, 1, d), lambda i, c: (layer, 0, 0)),
            pl.BlockSpec((None, d, tn), lambda i, c: (j, 0, c)),
            pl.BlockSpec((None, d, tn), lambda i, c: (j, 0, nd + c)),
            pl.BlockSpec((None, d, tn), lambda i, c: (j, 0, 2 * nd + c)),
            pl.BlockSpec((None, kw, tn), lambda i, c: (j, 0, c)),
            pl.BlockSpec((None, tn, d), lambda i, c: (j, c, 0)),
        ],
        out_specs=pl.BlockSpec((tm, d), lambda i, c: (i, 0)),
        out_shape=jax.ShapeDtypeStruct((seq, d), F32),
        scratch_shapes=[pltpu.VMEM((tm + 2 * HALO, d), BF),
                        pltpu.VMEM((tm + 2 * HALO, tn), F32)],
        compiler_params=_params(48),
        name="short_conv",
    )(x, x, x, g, w_in, w_in, w_in, taps, w_out)


def _cf_kernel(xp_ref, x_ref, xn_ref, g_ref, wa_ref, wg_ref, ba_ref, bg_ref, taps_ref, bdw_ref,
               lng_ref, lnb_ref, w2_ref, b2_ref, o_ref, h_ref, a_ref, shift_ref, conv_ref, *, tm, seq):
    c = pl.program_id(1)
    nd = pl.num_programs(1)

    @pl.when(c == 0)
    def _():
        _fill_normed(h_ref, xp_ref, x_ref, xn_ref, g_ref[...], tm)

    h = h_ref[...]
    inside = _in_sequence(tm, seq)
    sub, tw = a_ref.shape[0], a_ref.shape[2]
    for s in range(sub):
        cols = pl.ds(s * tw, tw)
        a = (_dot(h, wa_ref[:, cols]) + ba_ref[:, cols]) * jax.nn.sigmoid(_dot(h, wg_ref[:, cols]) + bg_ref[:, cols])
        a_ref[s] = jnp.where(inside, a, 0.0)
    for s in range(sub):
        cols = pl.ds(s * tw, tw)
        conv_ref[c * sub + s] = _dwconv_wide(a_ref.at[s], shift_ref, taps_ref[:, cols], tm) + bdw_ref[:, cols]

    @pl.when(c == nd - 1)
    def _():
        chunks = conv_ref.shape[0]
        d = chunks * tw
        total = jnp.zeros((tm, 1), F32)
        for k in range(chunks):
            total = total + jnp.sum(conv_ref[k], axis=-1, keepdims=True)
        mu = total / d
        sq = jnp.zeros((tm, 1), F32)
        for k in range(chunks):
            xc = conv_ref[k] - mu
            sq = sq + jnp.sum(xc * xc, axis=-1, keepdims=True)
        rstd = lax.rsqrt(sq / d + LN_EPS)
        o_ref[...] = x_ref[...] + b2_ref[...]
        for k in range(chunks):
            y = (conv_ref[k] - mu) * rstd * lng_ref[:, k * tw:(k + 1) * tw] + lnb_ref[:, k * tw:(k + 1) * tw]
            s = (y * jax.nn.sigmoid(y)).astype(BF)
            o_ref[...] += _dot(s, w2_ref[k * tw:(k + 1) * tw, :])


def _conformer_mixer(x, g, w1, b1, taps, bdw, lng, lnb, w2, b2, layer, j, *, tm, tn, tw):
    seq, d = x.shape
    nd = d // tn
    kw = taps.shape[1]
    span = (HALO + kw // 2) // V7X_SUBLANES * V7X_SUBLANES + tm
    row = lambda i, c: (j, 0, 0)
    return pl.pallas_call(
        functools.partial(_cf_kernel, tm=tm, seq=seq),
        grid=(seq // tm, nd),
        in_specs=_halo_specs(tm, d, seq) + [
            pl.BlockSpec((None, 1, d), lambda i, c: (layer, 0, 0)),
            pl.BlockSpec((None, d, tn), lambda i, c: (j, 0, c)),
            pl.BlockSpec((None, d, tn), lambda i, c: (j, 0, nd + c)),
            pl.BlockSpec((None, 1, tn), lambda i, c: (j, 0, c)),
            pl.BlockSpec((None, 1, tn), lambda i, c: (j, 0, nd + c)),
            pl.BlockSpec((None, kw, tn), lambda i, c: (j, 0, c)),
            pl.BlockSpec((None, 1, tn), lambda i, c: (j, 0, c)),
            pl.BlockSpec((None, 1, d), row),
            pl.BlockSpec((None, 1, d), row),
            _resident((None, d, d), row),
            pl.BlockSpec((None, 1, d), row),
        ],
        out_specs=pl.BlockSpec((tm, d), lambda i, c: (i, 0)),
        out_shape=jax.ShapeDtypeStruct((seq, d), F32),
        scratch_shapes=[pltpu.VMEM((tm + 2 * HALO, d), BF),
                        pltpu.VMEM((tn // tw, tm + 2 * HALO, tw), F32),
                        pltpu.VMEM((V7X_SUBLANES - 1, span, tw), F32),
                        pltpu.VMEM((d // tw, tm, tw), F32)],
        compiler_params=_params(48),
        name="conformer",
    )(x, x, x, g, w1, w1, b1, b1, taps, bdw, lng, lnb, w2, b2)


def _hy_in_kernel(xp_ref, x_ref, xn_ref, g_ref, w0_ref, w1_ref, w2_ref, t0_ref, t1_ref, t2_ref,
                  b0_ref, b1_ref, b2_ref, x0_ref, z_ref, h_ref, u_ref, *, tm, seq):
    @pl.when(pl.program_id(1) == 0)
    def _():
        _fill_normed(h_ref, xp_ref, x_ref, xn_ref, g_ref[...], tm)

    h = h_ref[...]
    inside = _in_sequence(tm, seq)

    def branch(w_ref, t_ref, b_ref):
        u_ref[...] = jnp.where(inside, _dot(h, w_ref[...]), 0.0)
        return _dwconv(u_ref, t_ref[...], tm) + b_ref[...]

    x0_ref[...] = branch(w0_ref, t0_ref, b0_ref).astype(BF)
    x1 = branch(w1_ref, t1_ref, b1_ref)
    z_ref[...] = (branch(w2_ref, t2_ref, b2_ref) * x1).astype(BF)


def _hyena_in(x, g, w_in, taps, bias, layer, j, *, tm, tn):
    seq, d = x.shape
    nd = d // tn
    kw = taps.shape[1]
    col = lambda k: (lambda i, c: (j, 0, k * nd + c))
    return pl.pallas_call(
        functools.partial(_hy_in_kernel, tm=tm, seq=seq),
        grid=(seq // tm, nd),
        in_specs=_halo_specs(tm, d, seq)
        + [pl.BlockSpec((None, 1, d), lambda i, c: (layer, 0, 0))]
        + [pl.BlockSpec((None, d, tn), col(k)) for k in range(3)]
        + [pl.BlockSpec((None, kw, tn), col(k)) for k in range(3)]
        + [pl.BlockSpec((None, 1, tn), col(k)) for k in range(3)],
        out_specs=[pl.BlockSpec((tm, tn), lambda i, c: (i, c))] * 2,
        out_shape=[jax.ShapeDtypeStruct((seq, d), BF)] * 2,
        scratch_shapes=[pltpu.VMEM((tm + 2 * HALO, d), BF),
                        pltpu.VMEM((tm + 2 * HALO, tn), F32)],
        compiler_params=_params(40),
        name="hyena_in",
    )(x, x, x, g, w_in, w_in, w_in, taps, taps, taps, bias, bias, bias)


GROUP = V7X_BF16_SUBLANES


def _dft_plan(seq):
    n = 2 * seq
    n2 = 256 if n >= 256 * 32 else 64
    n1 = n // n2
    h1 = n1 // 2
    assert h1 & (h1 - 1) == 0 and h1 >= V7X_BF16_SUBLANES, "sequence length must be a power of two"
    k1p = _round_up(h1 + 1, V7X_BF16_SUBLANES)
    pack = max(1, V7X_LANES // h1)
    return n, n1, n2, h1, k1p, pack


def _dft_tables(seq):
    n, n1, n2, h1, k1p, pack = _dft_plan(seq)
    khalf = _round_up(k1p, V7X_LANES)
    two_pi = 2.0 * math.pi

    k1 = np.arange(k1p)
    pos = np.arange(h1)[None, :] * n2 + np.arange(n2)[:, None]
    ang = ((k1[None, :, None] * pos[:, None, :]) % n) * (two_pi / n)
    fwd = np.concatenate([np.cos(ang), -np.sin(ang)], axis=1)
    fwd = fwd.reshape(n2 // pack, pack, 2 * k1p, 1, h1) * np.eye(pack)[None, :, None, :, None]
    fwd = fwd.reshape(n2 // pack, pack * 2 * k1p, pack * h1)

    weight = np.where(k1 == 0, 1.0, np.where(k1 < n1 // 2, 2.0, np.where(k1 == n1 // 2, 1.0, 0.0))) / n
    angt = np.swapaxes(ang, 1, 2)
    widen = ((0, 0), (0, 0), (0, khalf - k1p))
    inv = np.concatenate([np.pad(np.cos(angt) * weight, widen),
                          np.pad(-np.sin(angt) * weight, widen)], axis=2)

    idx = np.arange(n2)
    a2 = ((idx[:, None] * idx[None, :]) % n2) * (two_pi / n2)
    cr, ci = np.cos(a2), -np.sin(a2)
    mid_fwd = np.block([[cr, -ci], [ci, cr]])
    mid_inv = np.block([[cr, ci], [-ci, cr]])
    return tuple(jnp.asarray(t.astype(BF)) for t in (fwd, inv, mid_fwd, mid_inv))


def _group_positions(step, h1, n2, first=0, rows=None):
    rows = GROUP * h1 if rows is None else rows
    e = first + lax.broadcasted_iota(jnp.int32, (rows, 1), 0)
    q = lax.shift_right_logical(e, jnp.int32(int(math.log2(h1))))
    return (e - q * h1) * n2 + step * GROUP + q


def _hy_mlp_kernel(fpad_ref, w1_ref, b1_ref, w2_ref, b2_ref, w3_ref, b3_ref, fr_ref, a_ref,
                   *, h1, n2, seq, bands):
    row = _group_positions(pl.program_id(0), h1, n2).astype(F32)
    t = row / (seq - 1)
    arg = ((2.0 * math.pi / seq) * row) * fpad_ref[...]
    lane = lax.broadcasted_iota(jnp.int32, (1, fpad_ref.shape[1]), 1)
    feat = jnp.where(lane == 0, t,
                     jnp.where(lane <= bands, jnp.cos(arg),
                               jnp.where(lane <= 2 * bands, -jnp.sin(arg), 0.0)))
    fr = fr_ref[...]
    a = jnp.sin(fr * (_dot_hi(feat, w1_ref[...]) + b1_ref[...]))
    a = jnp.sin(fr * (_dot_hi(a, w2_ref[...]) + b2_ref[...]))
    a = jnp.sin(fr * (_dot_hi(a, w3_ref[...]) + b3_ref[...]))
    a_ref[...] = a.astype(BF)


def _hyena_filter_features(seq, fpad, w1, b1, w2, b2, w3, b3, fr, bands):
    n, n1, n2, h1, k1p, pack = _dft_plan(seq)
    wp = w1.shape[1]
    full = lambda a: pl.BlockSpec(a.shape, lambda s: (0,) * a.ndim)
    args = (fpad, w1, b1, w2, b2, w3, b3, fr)
    return pl.pallas_call(
        functools.partial(_hy_mlp_kernel, h1=h1, n2=n2, seq=seq, bands=bands),
        grid=(n2 // GROUP,),
        in_specs=[full(a) for a in args],
        out_specs=pl.BlockSpec((GROUP * h1, wp), lambda s: (s, 0)),
        out_shape=jax.ShapeDtypeStruct((seq, wp), BF),
        compiler_params=_params(32),
        name="hyena_filter_mlp",
    )(*args)


def _dft_a_kernel(tab_ref, z_ref, a3_ref, w4f_ref, w4b_ref, dl_ref, az_ref, af_ref, ab_ref, ssq_ref,
                  zt_ref, stage_ref, *, h1, n2, k1p, pack, seq):
    step = pl.program_id(1)
    cw = z_ref.shape[-1]
    kin = pack * h1

    @pl.when(step == 0)
    def _():
        ssq_ref[...] = jnp.zeros_like(ssq_ref)

    zt_ref[...] = pltpu.einshape("abc->bac", z_ref[...]).reshape(GROUP * h1, cw)
    ssq = jnp.zeros((1, cw), F32)
    for p in range(GROUP // pack):
        rows = pl.ds(p * kin, kin)
        pos = _group_positions(step, h1, n2, p * kin, kin)
        decay = jnp.exp(-(pos.astype(F32) / (seq - 1)) * dl_ref[...])
        a3 = a3_ref[rows, :]
        hf = _dot(a3, w4f_ref[...]) * decay
        hb = jnp.where(pos == 0, 0.0, _dot(a3, w4b_ref[...]) * decay)
        ssq = ssq + jnp.sum(hf * hf + hb * hb, axis=0, keepdims=True)
        tab = tab_ref[p]
        for s, src in enumerate((zt_ref[rows, :], hf.astype(BF), hb.astype(BF))):
            a = _dot(tab, src)
            stage_ref[s, pl.ds(p * pack, pack)] = a.astype(BF).reshape(pack, 2 * k1p, cw)
    ssq_ref[...] += ssq
    for s, dst in enumerate((az_ref, af_ref, ab_ref)):
        dst[...] = pltpu.einshape("qkc->kqc", stage_ref[s])


def _dft_stage_a(tab, z, a3, w4, deltas, seq, d, *, cw):
    n, n1, n2, h1, k1p, pack = _dft_plan(seq)
    wp = a3.shape[1]
    nc = d // cw
    spectrum = pl.BlockSpec((2 * k1p, GROUP, cw), lambda c, s: (0, s, c))
    shape = jax.ShapeDtypeStruct((2 * k1p, n2, d), BF)
    return pl.pallas_call(
        functools.partial(_dft_a_kernel, h1=h1, n2=n2, k1p=k1p, pack=pack, seq=seq),
        grid=(nc, n2 // GROUP),
        in_specs=[pl.BlockSpec((GROUP // pack,) + tab.shape[1:], lambda c, s: (s, 0, 0)),
                  pl.BlockSpec((h1, GROUP, cw), lambda c, s: (0, s, c)),
                  pl.BlockSpec((GROUP * h1, wp), lambda c, s: (s, 0)),
                  pl.BlockSpec((wp, cw), lambda c, s: (0, c)),
                  pl.BlockSpec((wp, cw), lambda c, s: (0, nc + c)),
                  pl.BlockSpec((1, cw), lambda c, s: (0, c))],
        out_specs=[spectrum] * 3 + [pl.BlockSpec((1, cw), lambda c, s: (0, c))],
        out_shape=[shape] * 3 + [jax.ShapeDtypeStruct((1, d), F32)],
        scratch_shapes=[pltpu.VMEM((GROUP * h1, cw), BF),
                        pltpu.VMEM((3, GROUP, 2 * k1p, cw), BF)],
        compiler_params=_params(48),
        name="dft_stage_a",
    )(tab, z.reshape(h1, n2, d), a3, w4, w4, deltas)


def _dft_b_kernel(mf_ref, mi_ref, az_ref, af_ref, ab_ref, s_ref, *, n2, d, cw, nyquist):
    @pl.when(pl.program_id(0) > nyquist)
    def _():
        s_ref[...] = jnp.zeros_like(s_ref)

    @pl.when(pl.program_id(0) <= nyquist)
    def _():
        mf = mf_ref[...]
        mi = mi_ref[...]
        for c in range(d // cw):
            cols = pl.ds(c * cw, cw)
            xz = _dot(mf, az_ref[:, 0, :, cols].reshape(2 * n2, cw))
            xf = _dot(mf, af_ref[:, 0, :, cols].reshape(2 * n2, cw))
            xb = _dot(mf, ab_ref[:, 0, :, cols].reshape(2 * n2, cw))
            kr = xf[:n2] + xb[:n2]
            ki = xf[n2:] - xb[n2:]
            yr = xz[:n2] * kr - xz[n2:] * ki
            yi = xz[:n2] * ki + xz[n2:] * kr
            s = _dot(mi, jnp.concatenate([yr, yi], axis=0).astype(BF))
            s_ref[0, 0, :, cols] = s[:n2].astype(BF)
            s_ref[1, 0, :, cols] = s[n2:].astype(BF)


def _dft_stage_b(mid_fwd, mid_inv, az, af, ab, seq, d, *, cw):
    n, n1, n2, h1, k1p, pack = _dft_plan(seq)
    view = lambda a: a.reshape(2, k1p, n2, d)
    blk = pl.BlockSpec((2, 1, n2, d), lambda k: (0, k, 0, 0))
    src = pl.BlockSpec((2, 1, n2, d), lambda k: (0, jnp.minimum(k, h1), 0, 0))
    mat = _resident((2 * n2, 2 * n2), lambda k: (0, 0))
    return pl.pallas_call(
        functools.partial(_dft_b_kernel, n2=n2, d=d, cw=cw, nyquist=h1),
        grid=(k1p,),
        in_specs=[mat, mat, src, src, src],
        out_specs=blk,
        out_shape=jax.ShapeDtypeStruct((2, k1p, n2, d), BF),
        compiler_params=_params(48),
        name="dft_stage_b",
    )(mid_fwd, mid_inv, view(az), view(af), view(ab))


def _dft_c_kernel(tab_ref, s_ref, y_ref, pad_ref, st_ref, ys_ref, *, k1p):
    khalf = pad_ref.shape[0] // 2

    @pl.when(jnp.logical_and(pl.program_id(0) == 0, pl.program_id(1) == 0))
    def _():
        pad_ref[...] = jnp.zeros_like(pad_ref)

    st_ref[...] = pltpu.einshape("kqc->qkc", s_ref[...])
    for q in range(GROUP):
        pad_ref[pl.ds(0, k1p), :] = st_ref[q, pl.ds(0, k1p), :]
        pad_ref[pl.ds(khalf, k1p), :] = st_ref[q, pl.ds(k1p, k1p), :]
        ys_ref[q] = _dot(tab_ref[q], pad_ref[...]).astype(BF)
    y_ref[...] = pltpu.einshape("qnc->nqc", ys_ref[...])


def _dft_stage_c(tab, s, seq, d, *, cw):
    n, n1, n2, h1, k1p, pack = _dft_plan(seq)
    kc = tab.shape[2]
    y = pl.pallas_call(
        functools.partial(_dft_c_kernel, k1p=k1p),
        grid=(d // cw, n2 // GROUP),
        in_specs=[pl.BlockSpec((GROUP, h1, kc), lambda c, q: (q, 0, 0)),
                  pl.BlockSpec((2 * k1p, GROUP, cw), lambda c, q: (0, q, c))],
        out_specs=pl.BlockSpec((h1, GROUP, cw), lambda c, q: (0, q, c)),
        out_shape=jax.ShapeDtypeStruct((h1, n2, d), BF),
        scratch_shapes=[pltpu.VMEM((kc, cw), BF),
                        pltpu.VMEM((GROUP, 2 * k1p, cw), BF),
                        pltpu.VMEM((GROUP, h1, cw), BF)],
        compiler_params=_params(40),
        name="dft_stage_c",
    )(tab, s.reshape(2 * k1p, n2, d))
    return y.reshape(seq, d)


def _hy_out_kernel(x_ref, y_ref, z_ref, x0_ref, ssq_ref, skip_ref, wo_ref, o_ref):
    y = y_ref[...].astype(F32) * lax.rsqrt(ssq_ref[...]) + z_ref[...].astype(F32) * skip_ref[...]
    gated = (y * x0_ref[...].astype(F32)).astype(BF)
    o_ref[...] = x_ref[...] + _dot(gated, wo_ref[...])


def _hyena_out(x, y, z, x0, ssq, skip, w_out, j, *, tm):
    seq, d = x.shape
    tile = pl.BlockSpec((tm, d), lambda i: (i, 0))
    return pl.pallas_call(
        _hy_out_kernel,
        grid=(seq // tm,),
        in_specs=[tile, tile, tile, tile,
                  pl.BlockSpec((1, d), lambda i: (0, 0)),
                  pl.BlockSpec((None, 1, d), lambda i: (j, 0, 0)),
                  _resident((None, d, d), lambda i: (j, 0, 0))],
        out_specs=tile,
        out_shape=jax.ShapeDtypeStruct((seq, d), F32),
        compiler_params=_params(40),
        name="hyena_out",
    )(x, y, z, x0, ssq, skip, w_out)


def _hyena_mixer(x, g, w_in, taps, bias, f_w1, f_b1, f_w2, f_b2, f_w3, f_b3, f_w4, freq, skip, w_out,
                 layer, j, *, tm, tn):
    seq, d = x.shape
    x0, z = _hyena_in(x, g, w_in, taps, bias, layer, j, tm=tm, tn=tn)

    emb, order = f_w1.shape[1], f_w1.shape[2]
    bands = (emb - 1) // 2
    wp = _round_up(order, V7X_LANES)
    ep = _round_up(emb, V7X_LANES)
    pad2 = lambda a, r, c: jnp.pad(a.astype(F32), ((0, r - a.shape[0]), (0, c - a.shape[1])))
    freqs = jnp.linspace(1e-4, bands - 1, bands, dtype=F32)
    fpad = pad2(jnp.concatenate([jnp.zeros((1,), F32), freqs, freqs])[None, :], 1, ep)
    deltas = jnp.abs(jnp.linspace(math.log(HY_TARGET) / HY_FAST_DECAY_PCT,
                                  math.log(HY_TARGET) / HY_SLOW_DECAY_PCT, d, dtype=F32))[None, :]
    a3 = _hyena_filter_features(
        seq, fpad,
        pad2(f_w1[j], ep, wp), pad2(f_b1[j][None, :], 1, wp),
        pad2(f_w2[j], wp, wp), pad2(f_b2[j][None, :], 1, wp),
        pad2(f_w3[j], wp, wp), pad2(f_b3[j][None, :], 1, wp),
        pad2(freq[j][None, :], 1, wp), bands)

    cw = min(512, d)
    tab_a, tab_c, mid_fwd, mid_inv = _dft_tables(seq)
    az, af, ab, ssq = _dft_stage_a(tab_a, z, a3, pad2(f_w4[j], wp, 2 * d).astype(BF), deltas, seq, d, cw=cw)
    s = _dft_stage_b(mid_fwd, mid_inv, az, af, ab, seq, d, cw=cw)
    y = _dft_stage_c(tab_c, s, seq, d, cw=cw)
    return _hyena_out(x, y, z, x0, ssq, skip, w_out, j, tm=min(256, seq))


def kernel(x, p, norm_ffn1, ffn1_w_gate, ffn1_w_up, ffn1_w_down, norm_mix, norm_ffn2, ffn2_w_gate, ffn2_w_up, ffn2_w_down, norm_ple, ple_w_gate, ple_w_up, sc_w_in, sc_conv, sc_w_out, cf_w_pw1, cf_b_pw1, cf_dw, cf_b_dw, cf_ln_g, cf_ln_b, cf_w_pw2, cf_b_pw2, hy_w_in, hy_short, hy_b_short, hy_f_w1, hy_f_b1, hy_f_w2, hy_f_b2, hy_f_w3, hy_f_b3, hy_f_w4, hy_freq, hy_skip, hy_w_out, norm_final):
    batch, seq, d = x.shape
    depth = p.shape[0]
    assert batch == 1, "the row tiling treats the sequence as the only token axis"
    tm = min(512, seq)
    tn = min(512, d)
    tn_cf = min(256, d)
    tf = min(512, ffn1_w_gate.shape[-1])
    tm_ffn = min(1024, seq)

    bf = lambda a: a.astype(BF)
    row = lambda a: a[:, None, :].astype(F32)
    ffn1_w = (ffn1_w_gate, ffn1_w_up, ffn1_w_down)
    ffn2_w = (ffn2_w_gate, ffn2_w_up, ffn2_w_down)
    w_cur = tuple(bf(w[0]) for w in ffn1_w)
    n_ffn1, n_mix, n_ffn2, n_ple = row(norm_ffn1), row(norm_mix), row(norm_ffn2), row(norm_ple)
    ple_gate, ple_up = bf(ple_w_gate), bf(ple_w_up)
    sc_in, sc_out = bf(sc_w_in), bf(sc_w_out)
    cf_w1, cf_w2 = bf(cf_w_pw1), bf(cf_w_pw2)
    hy_in, hy_out = bf(hy_w_in), bf(hy_w_out)
    gf = norm_final[None, :].astype(F32)

    xs = x.reshape(seq, d)
    ps = p.reshape(depth, seq, p.shape[-1])
    for i in range(depth):
        m, j = i % 3, i // 3
        xs, w_nxt = _ffn(xs, n_ffn1, *w_cur, i, ffn2_w + (i,), tm=tm_ffn, tf=tf)
        if m == 0:
            xs = _short_conv_mixer(xs, n_mix, sc_in, sc_conv.astype(F32), sc_out, i, j, tm=tm, tn=tn)
        elif m == 1:
            xs = _conformer_mixer(xs, n_mix, cf_w1, row(cf_b_pw1), cf_dw.astype(F32), row(cf_b_dw),
                                  row(cf_ln_g), row(cf_ln_b), cf_w2, row(cf_b_pw2), i, j, tm=tm, tn=tn_cf, tw=tn_cf)
        else:
            xs = _hyena_mixer(xs, n_mix, hy_in, hy_short.astype(F32), row(hy_b_short),
                              hy_f_w1, hy_f_b1, hy_f_w2, hy_f_b2, hy_f_w3, hy_f_b3, hy_f_w4,
                              hy_freq, row(hy_skip), hy_out, i, j, tm=tm, tn=tn)
        xs, w_cur = _ffn(xs, n_ffn2, *w_nxt, i, ffn1_w + (i + 1,) if i + 1 < depth else None, tm=tm_ffn, tf=tf)
        xs = _ple(xs, ps, n_ple, ple_gate, ple_up, gf, i, tm=tm, final=(i == depth - 1))
    return xs.reshape(batch, seq, d)
```

```python
import functools
import math

import jax
import jax.numpy as jnp
import numpy as np
from jax import lax
from jax.experimental import pallas as pl
from jax.experimental.pallas import tpu as pltpu

BF = jnp.bfloat16
F32 = jnp.float32

RMS_EPS = 1e-6
LN_EPS = 1e-5
HY_FAST_DECAY_PCT = 0.3
HY_SLOW_DECAY_PCT = 1.5
HY_TARGET = 1e-2

V7X_LANES = 128
V7X_SUBLANES = 8
V7X_BF16_SUBLANES = 16
HALO = V7X_BF16_SUBLANES
FFN_X_PARTS = 4
MIB = 1024 * 1024


def _dot(a, b):
    return jnp.dot(a, b, preferred_element_type=F32)


def _dot_hi(a, b):
    return jnp.dot(a, b, preferred_element_type=F32, precision=lax.Precision.HIGHEST)


def _rms(x, g):
    return x * lax.rsqrt(jnp.mean(x * x, axis=-1, keepdims=True) + RMS_EPS) * g


def _round_up(n, m):
    return (n + m - 1) // m * m


def _params(vmem_mib):
    return pltpu.CompilerParams(vmem_limit_bytes=int(vmem_mib * MIB))


def _resident(shape, index_map):
    return pl.BlockSpec(shape, index_map, pipeline_mode=pl.Buffered(1))


def _ffn_kernel(x_ref, g_ref, wg_ref, wu_ref, wd_ref, *rest, cast_blocks):
    if cast_blocks is None:
        o_ref, h_ref = rest
    else:
        ng_ref, nu_ref, nd_ref, o_ref, og_ref, ou_ref, od_ref, h_ref = rest
        step = pl.program_id(0) * pl.num_programs(1) + pl.program_id(1)

        @pl.when(step < cast_blocks[0])
        def _():
            og_ref[...] = ng_ref[...].astype(BF)
            ou_ref[...] = nu_ref[...].astype(BF)

        @pl.when(step < cast_blocks[1])
        def _():
            od_ref[...] = nd_ref[...].astype(BF)

    j = pl.program_id(1)
    rows = x_ref.shape[0]
    for part in range(FFN_X_PARTS):
        @pl.when(j == part)
        def _():
            x = x_ref[...]
            h_ref[pl.ds(part * rows, rows), :] = _rms(x, g_ref[...]).astype(BF)
            o_ref[pl.ds(part * rows, rows), :] = x

    @pl.when(j >= FFN_X_PARTS - 1)
    def _():
        h = h_ref[...]
        a = _dot(h, wg_ref[...])
        u = _dot(h, wu_ref[...])
        act = (a * jax.nn.sigmoid(a) * u * 0.5).astype(BF)
        o_ref[...] += _dot(act, wd_ref[...])


def _cast_rows(rows, steps):
    r = _round_up(-(-rows // steps), V7X_BF16_SUBLANES)
    while rows % r:
        r += V7X_BF16_SUBLANES
    return r


def _ffn(x, g, wg, wu, wd, layer, nxt, *, tm, tf):
    seq, d = x.shape
    f = wg.shape[-1]
    lead = FFN_X_PARTS - 1
    nj = f // tf + lead
    grid = (seq // tm, nj)
    chunk = lambda j: jnp.maximum(j - lead, 0)
    in_specs = [
        pl.BlockSpec((tm // FFN_X_PARTS, d), lambda i, j: (FFN_X_PARTS * i + jnp.minimum(j, lead), 0)),
        pl.BlockSpec((None, 1, d), lambda i, j: (layer, 0, 0)),
        pl.BlockSpec((d, tf), lambda i, j: (0, chunk(j))),
        pl.BlockSpec((d, tf), lambda i, j: (0, chunk(j))),
        pl.BlockSpec((tf, d), lambda i, j: (chunk(j), 0)),
    ]
    out_specs = [pl.BlockSpec((tm, d), lambda i, j: (i, 0))]
    out_shape = [jax.ShapeDtypeStruct((seq, d), F32)]
    args = [x, g, wg, wu, wd]
    cast_blocks = None
    if nxt is not None:
        ng, nu, nd, nl = nxt
        steps = grid[0] * grid[1]
        rg, rd = _cast_rows(d, steps), _cast_rows(f, steps)
        cast_blocks = (d // rg, f // rd)
        blk = lambda n: (lambda i, j: (jnp.minimum(i * nj + j, n - 1), 0))
        blk3 = lambda n: (lambda i, j: (nl, jnp.minimum(i * nj + j, n - 1), 0))
        in_specs += [pl.BlockSpec((None, rg, f), blk3(cast_blocks[0])),
                     pl.BlockSpec((None, rg, f), blk3(cast_blocks[0])),
                     pl.BlockSpec((None, rd, d), blk3(cast_blocks[1]))]
        out_specs += [pl.BlockSpec((rg, f), blk(cast_blocks[0])),
                      pl.BlockSpec((rg, f), blk(cast_blocks[0])),
                      pl.BlockSpec((rd, d), blk(cast_blocks[1]))]
        out_shape += [jax.ShapeDtypeStruct((d, f), BF), jax.ShapeDtypeStruct((d, f), BF),
                      jax.ShapeDtypeStruct((f, d), BF)]
        args += [ng, nu, nd]
    outs = pl.pallas_call(
        functools.partial(_ffn_kernel, cast_blocks=cast_blocks),
        grid=grid,
        in_specs=in_specs,
        out_specs=out_specs,
        out_shape=out_shape,
        scratch_shapes=[pltpu.VMEM((tm, d), BF)],
        compiler_params=_params(48),
        name="ffn",
    )(*args)
    return outs[0], (tuple(outs[1:]) if nxt is not None else None)


def _ple_kernel(x_ref, p_ref, g_ref, wgate_ref, wup_ref, gf_ref, o_ref, *, final):
    x = x_ref[...]
    h = _rms(x, g_ref[...]).astype(BF)
    gate = jax.nn.sigmoid(_dot(h, wgate_ref[...]))
    up = _dot(p_ref[...].astype(BF), wup_ref[...])
    y = x + up * gate
    if final:
        y = _rms(y, gf_ref[...])
    o_ref[...] = y


def _ple(x, p, g, wgate, wup, gf, layer, *, tm, final):
    seq, d = x.shape
    pd = p.shape[-1]
    return pl.pallas_call(
        functools.partial(_ple_kernel, final=final),
        grid=(seq // tm,),
        in_specs=[
            pl.BlockSpec((tm, d), lambda i: (i, 0)),
            pl.BlockSpec((None, tm, pd), lambda i: (layer, i, 0)),
            pl.BlockSpec((None, 1, d), lambda i: (layer, 0, 0)),
            _resident((None, d, d), lambda i: (layer, 0, 0)),
            _resident((None, pd, d), lambda i: (layer, 0, 0)),
            pl.BlockSpec((1, d), lambda i: (0, 0)),
        ],
        out_specs=pl.BlockSpec((tm, d), lambda i: (i, 0)),
        out_shape=jax.ShapeDtypeStruct((seq, d), F32),
        compiler_params=_params(48),
        name="ple",
    )(x, p, g, wgate, wup, gf)


def _halo_specs(tm, d, seq):
    per = tm // HALO
    last = seq // HALO - 1
    return [
        pl.BlockSpec((HALO, d), lambda i, j: (jnp.maximum(i * per - 1, 0), 0)),
        pl.BlockSpec((tm, d), lambda i, j: (i, 0)),
        pl.BlockSpec((HALO, d), lambda i, j: (jnp.minimum((i + 1) * per, last), 0)),
    ]


def _fill_normed(h_ref, xp_ref, x_ref, xn_ref, g, tm):
    h_ref[pl.ds(0, HALO), :] = _rms(xp_ref[...], g).astype(BF)
    h_ref[pl.ds(HALO, tm), :] = _rms(x_ref[...], g).astype(BF)
    h_ref[pl.ds(HALO + tm, HALO), :] = _rms(xn_ref[...], g).astype(BF)


def _in_sequence(tm, seq):
    row = pl.program_id(0) * tm - HALO + lax.broadcasted_iota(jnp.int32, (tm + 2 * HALO, 1), 0)
    return jnp.logical_and(row >= 0, row < seq)


def _dwconv(src_ref, taps, rows, start=0):
    k = taps.shape[0]
    first = HALO - k // 2 + start
    acc = taps[0:1] * src_ref[pl.ds(first, rows), :]
    for t in range(1, k):
        acc = acc + taps[t:t + 1] * src_ref[pl.ds(first + t, rows), :]
    return acc


CONV_ROWS = 32


def _dwconv_wide(src_ref, shift_ref, taps_ref, bias_ref, dst_ref, tm):
    k = taps_ref.shape[0]
    first = HALO - k // 2
    span = shift_ref.shape[1]
    ch = src_ref.shape[1]
    tiles = CONV_ROWS // V7X_SUBLANES
    for r in range(1, V7X_SUBLANES):
        shift_ref[r - 1] = src_ref[pl.ds(r, span), :]
    for blk in range(tm // CONV_ROWS):
        acc = None
        for t in range(k):
            base, r = divmod(first + t, V7X_SUBLANES)
            rows = pl.ds(base * V7X_SUBLANES + blk * CONV_ROWS, CONV_ROWS)
            src = src_ref[rows, :] if r == 0 else shift_ref[r - 1, rows, :]
            term = taps_ref[t][None] * src.reshape(tiles, V7X_SUBLANES, ch)
            acc = term if acc is None else acc + term
        dst_ref[pl.ds(blk * CONV_ROWS, CONV_ROWS), :] = acc.reshape(CONV_ROWS, ch) + bias_ref[...]


def _sc_kernel(xp_ref, x_ref, xn_ref, g_ref, wb_ref, wc_ref, wv_ref, taps_ref, wo_ref,
               o_ref, h_ref, cv_ref, *, tm, seq):
    @pl.when(pl.program_id(1) == 0)
    def _():
        _fill_normed(h_ref, xp_ref, x_ref, xn_ref, g_ref[...], tm)
        o_ref[...] = x_ref[...]

    h = h_ref[...]
    cv = _dot(h, wc_ref[...]) * _dot(h, wv_ref[...])
    cv_ref[...] = jnp.where(_in_sequence(tm, seq), cv, 0.0)
    conv = _dwconv(cv_ref, taps_ref[...], tm)
    b = _dot(h_ref[pl.ds(HALO, tm), :], wb_ref[...])
    o_ref[...] += _dot((b * conv).astype(BF), wo_ref[...])


def _short_conv_mixer(x, g, w_in, taps, w_out, layer, j, *, tm, tn):
    seq, d = x.shape
    nd = d // tn
    kw = taps.shape[1]
    return pl.pallas_call(
        functools.partial(_sc_kernel, tm=tm, seq=seq),
        grid=(seq // tm, nd),
        in_specs=_halo_specs(tm, d, seq) + [
            pl.BlockSpec((None, 1, d), lambda i, c: (layer, 0, 0)),
            pl.BlockSpec((None, d, tn), lambda i, c: (j, 0, c)),
            pl.BlockSpec((None, d, tn), lambda i, c: (j, 0, nd + c)),
            pl.BlockSpec((None, d, tn), lambda i, c: (j, 0, 2 * nd + c)),
            pl.BlockSpec((None, kw, tn), lambda i, c: (j, 0, c)),
            pl.BlockSpec((None, tn, d), lambda i, c: (j, c, 0)),
        ],
        out_specs=pl.BlockSpec((tm, d), lambda i, c: (i, 0)),
        out_shape=jax.ShapeDtypeStruct((seq, d), F32),
        scratch_shapes=[pltpu.VMEM((tm + 2 * HALO, d), BF),
                        pltpu.VMEM((tm + 2 * HALO, tn), F32)],
        compiler_params=_params(48),
        name="short_conv",
    )(x, x, x, g, w_in, w_in, w_in, taps, w_out)


def _cf_kernel(xp_ref, x_ref, xn_ref, g_ref, wa_ref, wg_ref, ba_ref, bg_ref, taps_ref, bdw_ref,
               lng_ref, lnb_ref, w2_ref, b2_ref, o_ref, h_ref, a0_ref, a1_ref, shift_ref, conv_ref, *, tm, seq):
    c = pl.program_id(1)
    nd = pl.num_programs(1) - 1
    tw = a0_ref.shape[1]

    @pl.when(c == 0)
    def _():
        _fill_normed(h_ref, xp_ref, x_ref, xn_ref, g_ref[...], tm)

    @pl.when(jnp.logical_and(pl.program_id(0) == 0, c == 0))
    def _():
        a1_ref[...] = jnp.zeros_like(a1_ref)

    def step(proj_ref, conv_src_ref):
        _dwconv_wide(conv_src_ref, shift_ref, taps_ref, bdw_ref, conv_ref.at[jnp.maximum(c - 1, 0)], tm)
        h = h_ref[...]
        a = (_dot(h, wa_ref[...]) + ba_ref[...]) * jax.nn.sigmoid(_dot(h, wg_ref[...]) + bg_ref[...])
        proj_ref[...] = jnp.where(_in_sequence(tm, seq), a, 0.0)

    @pl.when(c % 2 == 0)
    def _():
        step(a0_ref, a1_ref)

    @pl.when(c % 2 == 1)
    def _():
        step(a1_ref, a0_ref)

    @pl.when(c == nd)
    def _():
        chunks = conv_ref.shape[0]
        d = chunks * tw
        total = jnp.zeros((tm, 1), F32)
        for k in range(chunks):
            total = total + jnp.sum(conv_ref[k], axis=-1, keepdims=True)
        mu = total / d
        sq = jnp.zeros((tm, 1), F32)
        for k in range(chunks):
            xc = conv_ref[k] - mu
            sq = sq + jnp.sum(xc * xc, axis=-1, keepdims=True)
        rstd = lax.rsqrt(sq / d + LN_EPS)
        o_ref[...] = x_ref[...] + b2_ref[...]
        for k in range(chunks):
            y = (conv_ref[k] - mu) * rstd * lng_ref[:, k * tw:(k + 1) * tw] + lnb_ref[:, k * tw:(k + 1) * tw]
            s = (y * jax.nn.sigmoid(y)).astype(BF)
            o_ref[...] += _dot(s, w2_ref[k * tw:(k + 1) * tw, :])


def _conformer_mixer(x, g, w1, b1, taps, bdw, lng, lnb, w2, b2, layer, j, *, tm, tn):
    seq, d = x.shape
    nd = d // tn
    kw = taps.shape[1]
    span = (HALO + kw // 2) // V7X_SUBLANES * V7X_SUBLANES + tm
    row = lambda i, c: (j, 0, 0)
    proj = lambda c: jnp.minimum(c, nd - 1)
    conv = lambda c: jnp.maximum(c - 1, 0)
    return pl.pallas_call(
        functools.partial(_cf_kernel, tm=tm, seq=seq),
        grid=(seq // tm, nd + 1),
        in_specs=_halo_specs(tm, d, seq) + [
            pl.BlockSpec((None, 1, d), lambda i, c: (layer, 0, 0)),
            pl.BlockSpec((None, d, tn), lambda i, c: (j, 0, proj(c))),
            pl.BlockSpec((None, d, tn), lambda i, c: (j, 0, nd + proj(c))),
            pl.BlockSpec((None, 1, tn), lambda i, c: (j, 0, proj(c))),
            pl.BlockSpec((None, 1, tn), lambda i, c: (j, 0, nd + proj(c))),
            pl.BlockSpec((None, kw, V7X_SUBLANES, tn), lambda i, c: (j, 0, 0, conv(c))),
            pl.BlockSpec((None, 1, tn), lambda i, c: (j, 0, conv(c))),
            pl.BlockSpec((None, 1, d), row),
            pl.BlockSpec((None, 1, d), row),
            _resident((None, d, d), row),
            pl.BlockSpec((None, 1, d), row),
        ],
        out_specs=pl.BlockSpec((tm, d), lambda i, c: (i, 0)),
        out_shape=jax.ShapeDtypeStruct((seq, d), F32),
        scratch_shapes=[pltpu.VMEM((tm + 2 * HALO, d), BF),
                        pltpu.VMEM((tm + 2 * HALO, tn), F32),
                        pltpu.VMEM((tm + 2 * HALO, tn), F32),
                        pltpu.VMEM((V7X_SUBLANES - 1, span, tn), F32),
                        pltpu.VMEM((nd, tm, tn), F32)],
        compiler_params=_params(48),
        name="conformer",
    )(x, x, x, g, w1, w1, b1, b1, taps, bdw, lng, lnb, w2, b2)


def _hy_in_kernel(xp_ref, x_ref, xn_ref, g_ref, w0_ref, w1_ref, w2_ref, t0_ref, t1_ref, t2_ref,
                  b0_ref, b1_ref, b2_ref, x0_ref, z_ref, h_ref, u_ref, *, tm, seq):
    @pl.when(pl.program_id(1) == 0)
    def _():
        _fill_normed(h_ref, xp_ref, x_ref, xn_ref, g_ref[...], tm)

    h = h_ref[...]
    inside = _in_sequence(tm, seq)

    def branch(w_ref, t_ref, b_ref):
        u_ref[...] = jnp.where(inside, _dot(h, w_ref[...]), 0.0)
        return _dwconv(u_ref, t_ref[...], tm) + b_ref[...]

    x0_ref[...] = branch(w0_ref, t0_ref, b0_ref).astype(BF)
    x1 = branch(w1_ref, t1_ref, b1_ref)
    z_ref[...] = (branch(w2_ref, t2_ref, b2_ref) * x1).astype(BF)


def _hyena_in(x, g, w_in, taps, bias, layer, j, *, tm, tn):
    seq, d = x.shape
    nd = d // tn
    kw = taps.shape[1]
    col = lambda k: (lambda i, c: (j, 0, k * nd + c))
    return pl.pallas_call(
        functools.partial(_hy_in_kernel, tm=tm, seq=seq),
        grid=(seq // tm, nd),
        in_specs=_halo_specs(tm, d, seq)
        + [pl.BlockSpec((None, 1, d), lambda i, c: (layer, 0, 0))]
        + [pl.BlockSpec((None, d, tn), col(k)) for k in range(3)]
        + [pl.BlockSpec((None, kw, tn), col(k)) for k in range(3)]
        + [pl.BlockSpec((None, 1, tn), col(k)) for k in range(3)],
        out_specs=[pl.BlockSpec((tm, tn), lambda i, c: (i, c))] * 2,
        out_shape=[jax.ShapeDtypeStruct((seq, d), BF)] * 2,
        scratch_shapes=[pltpu.VMEM((tm + 2 * HALO, d), BF),
                        pltpu.VMEM((tm + 2 * HALO, tn), F32)],
        compiler_params=_params(40),
        name="hyena_in",
    )(x, x, x, g, w_in, w_in, w_in, taps, taps, taps, bias, bias, bias)


GROUP = V7X_BF16_SUBLANES


def _dft_plan(seq):
    n = 2 * seq
    n2 = 256 if n >= 256 * 32 else 64
    n1 = n // n2
    h1 = n1 // 2
    assert h1 & (h1 - 1) == 0 and h1 >= V7X_BF16_SUBLANES, "sequence length must be a power of two"
    k1p = _round_up(h1 + 1, V7X_BF16_SUBLANES)
    pack = max(1, V7X_LANES // h1)
    return n, n1, n2, h1, k1p, pack


def _dft_tables(seq):
    n, n1, n2, h1, k1p, pack = _dft_plan(seq)
    khalf = _round_up(k1p, V7X_LANES)
    two_pi = 2.0 * math.pi

    k1 = np.arange(k1p)
    pos = np.arange(h1)[None, :] * n2 + np.arange(n2)[:, None]
    ang = ((k1[None, :, None] * pos[:, None, :]) % n) * (two_pi / n)
    fwd = np.concatenate([np.cos(ang), -np.sin(ang)], axis=1)
    fwd = fwd.reshape(n2 // pack, pack, 2 * k1p, 1, h1) * np.eye(pack)[None, :, None, :, None]
    fwd = fwd.reshape(n2 // pack, pack * 2 * k1p, pack * h1)

    weight = np.where(k1 == 0, 1.0, np.where(k1 < n1 // 2, 2.0, np.where(k1 == n1 // 2, 1.0, 0.0))) / n
    angt = np.swapaxes(ang, 1, 2)
    widen = ((0, 0), (0, 0), (0, khalf - k1p))
    inv = np.concatenate([np.pad(np.cos(angt) * weight, widen),
                          np.pad(-np.sin(angt) * weight, widen)], axis=2)

    idx = np.arange(n2)
    a2 = ((idx[:, None] * idx[None, :]) % n2) * (two_pi / n2)
    cr, ci = np.cos(a2), -np.sin(a2)
    mid_fwd = np.block([[cr, -ci], [ci, cr]])
    mid_inv = np.block([[cr, ci], [-ci, cr]])
    return tuple(jnp.asarray(t.astype(BF)) for t in (fwd, inv, mid_fwd, mid_inv))


def _group_positions(step, h1, n2, first=0, rows=None):
    rows = GROUP * h1 if rows is None else rows
    e = first + lax.broadcasted_iota(jnp.int32, (rows, 1), 0)
    q = lax.shift_right_logical(e, jnp.int32(int(math.log2(h1))))
    return (e - q * h1) * n2 + step * GROUP + q


def _hy_mlp_kernel(fpad_ref, w1_ref, b1_ref, w2_ref, b2_ref, w3_ref, b3_ref, fr_ref, a_ref,
                   *, h1, n2, seq, bands):
    row = _group_positions(pl.program_id(0), h1, n2).astype(F32)
    t = row / (seq - 1)
    arg = ((2.0 * math.pi / seq) * row) * fpad_ref[...]
    lane = lax.broadcasted_iota(jnp.int32, (1, fpad_ref.shape[1]), 1)
    feat = jnp.where(lane == 0, t,
                     jnp.where(lane <= bands, jnp.cos(arg),
                               jnp.where(lane <= 2 * bands, -jnp.sin(arg), 0.0)))
    fr = fr_ref[...]
    a = jnp.sin(fr * (_dot_hi(feat, w1_ref[...]) + b1_ref[...]))
    a = jnp.sin(fr * (_dot_hi(a, w2_ref[...]) + b2_ref[...]))
    a = jnp.sin(fr * (_dot_hi(a, w3_ref[...]) + b3_ref[...]))
    a_ref[...] = a.astype(BF)


def _hyena_filter_features(seq, fpad, w1, b1, w2, b2, w3, b3, fr, bands):
    n, n1, n2, h1, k1p, pack = _dft_plan(seq)
    wp = w1.shape[1]
    full = lambda a: pl.BlockSpec(a.shape, lambda s: (0,) * a.ndim)
    args = (fpad, w1, b1, w2, b2, w3, b3, fr)
    return pl.pallas_call(
        functools.partial(_hy_mlp_kernel, h1=h1, n2=n2, seq=seq, bands=bands),
        grid=(n2 // GROUP,),
        in_specs=[full(a) for a in args],
        out_specs=pl.BlockSpec((GROUP * h1, wp), lambda s: (s, 0)),
        out_shape=jax.ShapeDtypeStruct((seq, wp), BF),
        compiler_params=_params(32),
        name="hyena_filter_mlp",
    )(*args)


def _dft_a_kernel(tab_ref, z_ref, a3_ref, w4f_ref, w4b_ref, dl_ref, az_ref, af_ref, ab_ref, ssq_ref,
                  zt_ref, stage_ref, *, h1, n2, k1p, pack, seq):
    step = pl.program_id(1)
    cw = z_ref.shape[-1]
    kin = pack * h1

    @pl.when(step == 0)
    def _():
        ssq_ref[...] = jnp.zeros_like(ssq_ref)

    zt_ref[...] = pltpu.einshape("abc->bac", z_ref[...]).reshape(GROUP * h1, cw)
    ssq = jnp.zeros((1, cw), F32)
    for p in range(GROUP // pack):
        rows = pl.ds(p * kin, kin)
        pos = _group_positions(step, h1, n2, p * kin, kin)
        decay = jnp.exp(-(pos.astype(F32) / (seq - 1)) * dl_ref[...])
        a3 = a3_ref[rows, :]
        hf = _dot(a3, w4f_ref[...]) * decay
        hb = jnp.where(pos == 0, 0.0, _dot(a3, w4b_ref[...]) * decay)
        ssq = ssq + jnp.sum(hf * hf + hb * hb, axis=0, keepdims=True)
        tab = tab_ref[p]
        for s, src in enumerate((zt_ref[rows, :], hf.astype(BF), hb.astype(BF))):
            a = _dot(tab, src)
            stage_ref[s, pl.ds(p * pack, pack)] = a.astype(BF).reshape(pack, 2 * k1p, cw)
    ssq_ref[...] += ssq
    for s, dst in enumerate((az_ref, af_ref, ab_ref)):
        dst[...] = pltpu.einshape("qkc->kqc", stage_ref[s])


def _dft_stage_a(tab, z, a3, w4, deltas, seq, d, *, cw):
    n, n1, n2, h1, k1p, pack = _dft_plan(seq)
    wp = a3.shape[1]
    nc = d // cw
    spectrum = pl.BlockSpec((2 * k1p, GROUP, cw), lambda c, s: (0, s, c))
    shape = jax.ShapeDtypeStruct((2 * k1p, n2, d), BF)
    return pl.pallas_call(
        functools.partial(_dft_a_kernel, h1=h1, n2=n2, k1p=k1p, pack=pack, seq=seq),
        grid=(nc, n2 // GROUP),
        in_specs=[pl.BlockSpec((GROUP // pack,) + tab.shape[1:], lambda c, s: (s, 0, 0)),
                  pl.BlockSpec((h1, GROUP, cw), lambda c, s: (0, s, c)),
                  pl.BlockSpec((GROUP * h1, wp), lambda c, s: (s, 0)),
                  pl.BlockSpec((wp, cw), lambda c, s: (0, c)),
                  pl.BlockSpec((wp, cw), lambda c, s: (0, nc + c)),
                  pl.BlockSpec((1, cw), lambda c, s: (0, c))],
        out_specs=[spectrum] * 3 + [pl.BlockSpec((1, cw), lambda c, s: (0, c))],
        out_shape=[shape] * 3 + [jax.ShapeDtypeStruct((1, d), F32)],
        scratch_shapes=[pltpu.VMEM((GROUP * h1, cw), BF),
                        pltpu.VMEM((3, GROUP, 2 * k1p, cw), BF)],
        compiler_params=_params(48),
        name="dft_stage_a",
    )(tab, z.reshape(h1, n2, d), a3, w4, w4, deltas)


def _dft_b_kernel(mf_ref, mi_ref, az_ref, af_ref, ab_ref, s_ref, *, n2, d, cw, nyquist):
    @pl.when(pl.program_id(0) > nyquist)
    def _():
        s_ref[...] = jnp.zeros_like(s_ref)

    @pl.when(pl.program_id(0) <= nyquist)
    def _():
        mf = mf_ref[...]
        mi = mi_ref[...]
        for c in range(d // cw):
            cols = pl.ds(c * cw, cw)
            xz = _dot(mf, az_ref[:, 0, :, cols].reshape(2 * n2, cw))
            xf = _dot(mf, af_ref[:, 0, :, cols].reshape(2 * n2, cw))
            xb = _dot(mf, ab_ref[:, 0, :, cols].reshape(2 * n2, cw))
            kr = xf[:n2] + xb[:n2]
            ki = xf[n2:] - xb[n2:]
            yr = xz[:n2] * kr - xz[n2:] * ki
            yi = xz[:n2] * ki + xz[n2:] * kr
            s = _dot(mi, jnp.concatenate([yr, yi], axis=0).astype(BF))
            s_ref[0, 0, :, cols] = s[:n2].astype(BF)
            s_ref[1, 0, :, cols] = s[n2:].astype(BF)


def _dft_stage_b(mid_fwd, mid_inv, az, af, ab, seq, d, *, cw):
    n, n1, n2, h1, k1p, pack = _dft_plan(seq)
    view = lambda a: a.reshape(2, k1p, n2, d)
    blk = pl.BlockSpec((2, 1, n2, d), lambda k: (0, k, 0, 0))
    src = pl.BlockSpec((2, 1, n2, d), lambda k: (0, jnp.minimum(k, h1), 0, 0))
    mat = _resident((2 * n2, 2 * n2), lambda k: (0, 0))
    return pl.pallas_call(
        functools.partial(_dft_b_kernel, n2=n2, d=d, cw=cw, nyquist=h1),
        grid=(k1p,),
        in_specs=[mat, mat, src, src, src],
        out_specs=blk,
        out_shape=jax.ShapeDtypeStruct((2, k1p, n2, d), BF),
        compiler_params=_params(48),
        name="dft_stage_b",
    )(mid_fwd, mid_inv, view(az), view(af), view(ab))


def _dft_c_kernel(tab_ref, s_ref, y_ref, pad_ref, st_ref, ys_ref, *, k1p):
    khalf = pad_ref.shape[0] // 2

    @pl.when(jnp.logical_and(pl.program_id(0) == 0, pl.program_id(1) == 0))
    def _():
        pad_ref[...] = jnp.zeros_like(pad_ref)

    st_ref[...] = pltpu.einshape("kqc->qkc", s_ref[...])
    for q in range(GROUP):
        pad_ref[pl.ds(0, k1p), :] = st_ref[q, pl.ds(0, k1p), :]
        pad_ref[pl.ds(khalf, k1p), :] = st_ref[q, pl.ds(k1p, k1p), :]
        ys_ref[q] = _dot(tab_ref[q], pad_ref[...]).astype(BF)
    y_ref[...] = pltpu.einshape("qnc->nqc", ys_ref[...])


def _dft_stage_c(tab, s, seq, d, *, cw):
    n, n1, n2, h1, k1p, pack = _dft_plan(seq)
    kc = tab.shape[2]
    y = pl.pallas_call(
        functools.partial(_dft_c_kernel, k1p=k1p),
        grid=(d // cw, n2 // GROUP),
        in_specs=[pl.BlockSpec((GROUP, h1, kc), lambda c, q: (q, 0, 0)),
                  pl.BlockSpec((2 * k1p, GROUP, cw), lambda c, q: (0, q, c))],
        out_specs=pl.BlockSpec((h1, GROUP, cw), lambda c, q: (0, q, c)),
        out_shape=jax.ShapeDtypeStruct((h1, n2, d), BF),
        scratch_shapes=[pltpu.VMEM((kc, cw), BF),
                        pltpu.VMEM((GROUP, 2 * k1p, cw), BF),
                        pltpu.VMEM((GROUP, h1, cw), BF)],
        compiler_params=_params(40),
        name="dft_stage_c",
    )(tab, s.reshape(2 * k1p, n2, d))
    return y.reshape(seq, d)


def _hy_out_kernel(x_ref, y_ref, z_ref, x0_ref, ssq_ref, skip_ref, wo_ref, o_ref):
    y = y_ref[...].astype(F32) * lax.rsqrt(ssq_ref[...]) + z_ref[...].astype(F32) * skip_ref[...]
    gated = (y * x0_ref[...].astype(F32)).astype(BF)
    o_ref[...] = x_ref[...] + _dot(gated, wo_ref[...])


def _hyena_out(x, y, z, x0, ssq, skip, w_out, j, *, tm):
    seq, d = x.shape
    tile = pl.BlockSpec((tm, d), lambda i: (i, 0))
    return pl.pallas_call(
        _hy_out_kernel,
        grid=(seq // tm,),
        in_specs=[tile, tile, tile, tile,
                  pl.BlockSpec((1, d), lambda i: (0, 0)),
                  pl.BlockSpec((None, 1, d), lambda i: (j, 0, 0)),
                  _resident((None, d, d), lambda i: (j, 0, 0))],
        out_specs=tile,
        out_shape=jax.ShapeDtypeStruct((seq, d), F32),
        compiler_params=_params(40),
        name="hyena_out",
    )(x, y, z, x0, ssq, skip, w_out)


def _hyena_mixer(x, g, w_in, taps, bias, f_w1, f_b1, f_w2, f_b2, f_w3, f_b3, f_w4, freq, skip, w_out,
                 layer, j, *, tm, tn):
    seq, d = x.shape
    x0, z = _hyena_in(x, g, w_in, taps, bias, layer, j, tm=tm, tn=tn)

    emb, order = f_w1.shape[1], f_w1.shape[2]
    bands = (emb - 1) // 2
    wp = _round_up(order, V7X_LANES)
    ep = _round_up(emb, V7X_LANES)
    pad2 = lambda a, r, c: jnp.pad(a.astype(F32), ((0, r - a.shape[0]), (0, c - a.shape[1])))
    freqs = jnp.linspace(1e-4, bands - 1, bands, dtype=F32)
    fpad = pad2(jnp.concatenate([jnp.zeros((1,), F32), freqs, freqs])[None, :], 1, ep)
    deltas = jnp.abs(jnp.linspace(math.log(HY_TARGET) / HY_FAST_DECAY_PCT,
                                  math.log(HY_TARGET) / HY_SLOW_DECAY_PCT, d, dtype=F32))[None, :]
    a3 = _hyena_filter_features(
        seq, fpad,
        pad2(f_w1[j], ep, wp), pad2(f_b1[j][None, :], 1, wp),
        pad2(f_w2[j], wp, wp), pad2(f_b2[j][None, :], 1, wp),
        pad2(f_w3[j], wp, wp), pad2(f_b3[j][None, :], 1, wp),
        pad2(freq[j][None, :], 1, wp), bands)

    cw = min(512, d)
    tab_a, tab_c, mid_fwd, mid_inv = _dft_tables(seq)
    az, af, ab, ssq = _dft_stage_a(tab_a, z, a3, pad2(f_w4[j], wp, 2 * d).astype(BF), deltas, seq, d, cw=cw)
    s = _dft_stage_b(mid_fwd, mid_inv, az, af, ab, seq, d, cw=cw)
    y = _dft_stage_c(tab_c, s, seq, d, cw=cw)
    return _hyena_out(x, y, z, x0, ssq, skip, w_out, j, tm=min(256, seq))


def kernel(x, p, norm_ffn1, ffn1_w_gate, ffn1_w_up, ffn1_w_down, norm_mix, norm_ffn2, ffn2_w_gate, ffn2_w_up, ffn2_w_down, norm_ple, ple_w_gate, ple_w_up, sc_w_in, sc_conv, sc_w_out, cf_w_pw1, cf_b_pw1, cf_dw, cf_b_dw, cf_ln_g, cf_ln_b, cf_w_pw2, cf_b_pw2, hy_w_in, hy_short, hy_b_short, hy_f_w1, hy_f_b1, hy_f_w2, hy_f_b2, hy_f_w3, hy_f_b3, hy_f_w4, hy_freq, hy_skip, hy_w_out, norm_final):
    batch, seq, d = x.shape
    depth = p.shape[0]
    assert batch == 1, "the row tiling treats the sequence as the only token axis"
    tm = min(512, seq)
    tn = min(512, d)
    tn_cf = min(256, d)
    tf = min(512, ffn1_w_gate.shape[-1])
    tm_ffn = min(1024, seq)

    bf = lambda a: a.astype(BF)
    row = lambda a: a[:, None, :].astype(F32)
    ffn1_w = (ffn1_w_gate, ffn1_w_up, ffn1_w_down)
    ffn2_w = (ffn2_w_gate, ffn2_w_up, ffn2_w_down)
    w_cur = tuple(bf(w[0]) for w in ffn1_w)
    n_ffn1, n_mix, n_ffn2, n_ple = row(norm_ffn1), row(norm_mix), row(norm_ffn2), row(norm_ple)
    ple_gate, ple_up = bf(ple_w_gate), bf(ple_w_up)
    sc_in, sc_out = bf(sc_w_in), bf(sc_w_out)
    cf_w1, cf_w2 = bf(cf_w_pw1), bf(cf_w_pw2)
    hy_in, hy_out = bf(hy_w_in), bf(hy_w_out)
    gf = norm_final[None, :].astype(F32)

    xs = x.reshape(seq, d)
    ps = p.reshape(depth, seq, p.shape[-1])
    for i in range(depth):
        m, j = i % 3, i // 3
        xs, w_nxt = _ffn(xs, n_ffn1, *w_cur, i, ffn2_w + (i,), tm=tm_ffn, tf=tf)
        if m == 0:
            xs = _short_conv_mixer(xs, n_mix, sc_in, sc_conv.astype(F32), sc_out, i, j, tm=tm, tn=tn)
        elif m == 1:
            cf_taps = jnp.broadcast_to(cf_dw.astype(F32)[:, :, None, :], cf_dw.shape[:2] + (V7X_SUBLANES, d))
            xs = _conformer_mixer(xs, n_mix, cf_w1, row(cf_b_pw1), cf_taps, row(cf_b_dw),
                                  row(cf_ln_g), row(cf_ln_b), cf_w2, row(cf_b_pw2), i, j, tm=tm, tn=tn_cf)
        else:
            xs = _hyena_mixer(xs, n_mix, hy_in, hy_short.astype(F32), row(hy_b_short),
                              hy_f_w1, hy_f_b1, hy_f_w2, hy_f_b2, hy_f_w3, hy_f_b3, hy_f_w4,
                              hy_freq, row(hy_skip), hy_out, i, j, tm=tm, tn=tn)
        xs, w_cur = _ffn(xs, n_ffn2, *w_nxt, i, ffn1_w + (i + 1,) if i + 1 < depth else None, tm=tm_ffn, tf=tf)
        xs = _ple(xs, ps, n_ple, ple_gate, ple_up, gf, i, tm=tm, final=(i == depth - 1))
    return xs.reshape(batch, seq, d)
```

```python
import functools
import math

import jax
import jax.numpy as jnp
import numpy as np
from jax import lax
from jax.experimental import pallas as pl
from jax.experimental.pallas import tpu as pltpu

BF = jnp.bfloat16
F32 = jnp.float32

RMS_EPS = 1e-6
LN_EPS = 1e-5
HY_FAST_DECAY_PCT = 0.3
HY_SLOW_DECAY_PCT = 1.5
HY_TARGET = 1e-2

V7X_LANES = 128
V7X_SUBLANES = 8
V7X_BF16_SUBLANES = 16
HALO = V7X_BF16_SUBLANES
FFN_X_PARTS = 4
MIB = 1024 * 1024


def _dot(a, b):
    return jnp.dot(a, b, preferred_element_type=F32)


def _dot_hi(a, b):
    return jnp.dot(a, b, preferred_element_type=F32, precision=lax.Precision.HIGHEST)


def _rms(x, g):
    return x * lax.rsqrt(jnp.mean(x * x, axis=-1, keepdims=True) + RMS_EPS) * g


def _round_up(n, m):
    return (n + m - 1) // m * m


def _params(vmem_mib):
    return pltpu.CompilerParams(vmem_limit_bytes=int(vmem_mib * MIB))


def _resident(shape, index_map):
    return pl.BlockSpec(shape, index_map, pipeline_mode=pl.Buffered(1))


def _ffn_kernel(x_ref, g_ref, wg_ref, wu_ref, wd_ref, *rest, cast_blocks):
    if cast_blocks is None:
        o_ref, h_ref = rest
    else:
        ng_ref, nu_ref, nd_ref, o_ref, og_ref, ou_ref, od_ref, h_ref = rest
        step = pl.program_id(0) * pl.num_programs(1) + pl.program_id(1)

        @pl.when(step < cast_blocks[0])
        def _():
            og_ref[...] = ng_ref[...].astype(BF)
            ou_ref[...] = nu_ref[...].astype(BF)

        @pl.when(step < cast_blocks[1])
        def _():
            od_ref[...] = nd_ref[...].astype(BF)

    j = pl.program_id(1)
    rows = x_ref.shape[0]
    for part in range(FFN_X_PARTS):
        @pl.when(j == part)
        def _():
            x = x_ref[...]
            h_ref[pl.ds(part * rows, rows), :] = _rms(x, g_ref[...]).astype(BF)
            o_ref[pl.ds(part * rows, rows), :] = x

    @pl.when(j >= FFN_X_PARTS - 1)
    def _():
        h = h_ref[...]
        a = _dot(h, wg_ref[...])
        u = _dot(h, wu_ref[...])
        act = (a * jax.nn.sigmoid(a) * u * 0.5).astype(BF)
        o_ref[...] += _dot(act, wd_ref[...])


def _cast_rows(rows, steps):
    r = _round_up(-(-rows // steps), V7X_BF16_SUBLANES)
    while rows % r:
        r += V7X_BF16_SUBLANES
    return r


def _ffn(x, g, wg, wu, wd, layer, nxt, *, tm, tf):
    seq, d = x.shape
    f = wg.shape[-1]
    lead = FFN_X_PARTS - 1
    nj = f // tf + lead
    grid = (seq // tm, nj)
    chunk = lambda j: jnp.maximum(j - lead, 0)
    in_specs = [
        pl.BlockSpec((tm // FFN_X_PARTS, d), lambda i, j: (FFN_X_PARTS * i + jnp.minimum(j, lead), 0)),
        pl.BlockSpec((None, 1, d), lambda i, j: (layer, 0, 0)),
        pl.BlockSpec((d, tf), lambda i, j: (0, chunk(j))),
        pl.BlockSpec((d, tf), lambda i, j: (0, chunk(j))),
        pl.BlockSpec((tf, d), lambda i, j: (chunk(j), 0)),
    ]
    out_specs = [pl.BlockSpec((tm, d), lambda i, j: (i, 0))]
    out_shape = [jax.ShapeDtypeStruct((seq, d), F32)]
    args = [x, g, wg, wu, wd]
    cast_blocks = None
    if nxt is not None:
        ng, nu, nd, nl = nxt
        steps = grid[0] * grid[1]
        rg, rd = _cast_rows(d, steps), _cast_rows(f, steps)
        cast_blocks = (d // rg, f // rd)
        blk = lambda n: (lambda i, j: (jnp.minimum(i * nj + j, n - 1), 0))
        blk3 = lambda n: (lambda i, j: (nl, jnp.minimum(i * nj + j, n - 1), 0))
        in_specs += [pl.BlockSpec((None, rg, f), blk3(cast_blocks[0])),
                     pl.BlockSpec((None, rg, f), blk3(cast_blocks[0])),
                     pl.BlockSpec((None, rd, d), blk3(cast_blocks[1]))]
        out_specs += [pl.BlockSpec((rg, f), blk(cast_blocks[0])),
                      pl.BlockSpec((rg, f), blk(cast_blocks[0])),
                      pl.BlockSpec((rd, d), blk(cast_blocks[1]))]
        out_shape += [jax.ShapeDtypeStruct((d, f), BF), jax.ShapeDtypeStruct((d, f), BF),
                      jax.ShapeDtypeStruct((f, d), BF)]
        args += [ng, nu, nd]
    outs = pl.pallas_call(
        functools.partial(_ffn_kernel, cast_blocks=cast_blocks),
        grid=grid,
        in_specs=in_specs,
        out_specs=out_specs,
        out_shape=out_shape,
        scratch_shapes=[pltpu.VMEM((tm, d), BF)],
        compiler_params=_params(48),
        name="ffn",
    )(*args)
    return outs[0], (tuple(outs[1:]) if nxt is not None else None)


def _ple_kernel(x_ref, p_ref, g_ref, wgate_ref, wup_ref, gf_ref, o_ref, *, final):
    x = x_ref[...]
    h = _rms(x, g_ref[...]).astype(BF)
    gate = jax.nn.sigmoid(_dot(h, wgate_ref[...]))
    up = _dot(p_ref[...].astype(BF), wup_ref[...])
    y = x + up * gate
    if final:
        y = _rms(y, gf_ref[...])
    o_ref[...] = y


def _ple(x, p, g, wgate, wup, gf, layer, *, tm, final):
    seq, d = x.shape
    pd = p.shape[-1]
    return pl.pallas_call(
        functools.partial(_ple_kernel, final=final),
        grid=(seq // tm,),
        in_specs=[
            pl.BlockSpec((tm, d), lambda i: (i, 0)),
            pl.BlockSpec((None, tm, pd), lambda i: (layer, i, 0)),
            pl.BlockSpec((None, 1, d), lambda i: (layer, 0, 0)),
            _resident((None, d, d), lambda i: (layer, 0, 0)),
            _resident((None, pd, d), lambda i: (layer, 0, 0)),
            pl.BlockSpec((1, d), lambda i: (0, 0)),
        ],
        out_specs=pl.BlockSpec((tm, d), lambda i: (i, 0)),
        out_shape=jax.ShapeDtypeStruct((seq, d), F32),
        compiler_params=_params(48),
        name="ple",
    )(x, p, g, wgate, wup, gf)


def _halo_specs(tm, d, seq):
    per = tm // HALO
    last = seq // HALO - 1
    return [
        pl.BlockSpec((HALO, d), lambda i, j: (jnp.maximum(i * per - 1, 0), 0)),
        pl.BlockSpec((tm, d), lambda i, j: (i, 0)),
        pl.BlockSpec((HALO, d), lambda i, j: (jnp.minimum((i + 1) * per, last), 0)),
    ]


def _fill_normed(h_ref, xp_ref, x_ref, xn_ref, g, tm):
    h_ref[pl.ds(0, HALO), :] = _rms(xp_ref[...], g).astype(BF)
    h_ref[pl.ds(HALO, tm), :] = _rms(x_ref[...], g).astype(BF)
    h_ref[pl.ds(HALO + tm, HALO), :] = _rms(xn_ref[...], g).astype(BF)


def _in_sequence(tm, seq):
    row = pl.program_id(0) * tm - HALO + lax.broadcasted_iota(jnp.int32, (tm + 2 * HALO, 1), 0)
    return jnp.logical_and(row >= 0, row < seq)


def _dwconv(src_ref, taps, rows, start=0):
    k = taps.shape[0]
    first = HALO - k // 2 + start
    acc = taps[0:1] * src_ref[pl.ds(first, rows), :]
    for t in range(1, k):
        acc = acc + taps[t:t + 1] * src_ref[pl.ds(first + t, rows), :]
    return acc


CONV_ROWS = 32


def _dwconv_wide(src_ref, shift_ref, taps_ref, bias_ref, dst_ref, tm):
    k = taps_ref.shape[0]
    first = HALO - k // 2
    span = shift_ref.shape[1]
    ch = src_ref.shape[1]
    tiles = CONV_ROWS // V7X_SUBLANES
    for r in range(1, V7X_SUBLANES):
        shift_ref[r - 1] = src_ref[pl.ds(r, span), :]
    for blk in range(tm // CONV_ROWS):
        acc = None
        for t in range(k):
            base, r = divmod(first + t, V7X_SUBLANES)
            rows = pl.ds(base * V7X_SUBLANES + blk * CONV_ROWS, CONV_ROWS)
            src = src_ref[rows, :] if r == 0 else shift_ref[r - 1, rows, :]
            term = taps_ref[t][None] * src.reshape(tiles, V7X_SUBLANES, ch)
            acc = term if acc is None else acc + term
        dst_ref[pl.ds(blk * CONV_ROWS, CONV_ROWS), :] = acc.reshape(CONV_ROWS, ch) + bias_ref[...]


def _sc_kernel(xp_ref, x_ref, xn_ref, g_ref, wb_ref, wc_ref, wv_ref, taps_ref, wo_ref,
               o_ref, h_ref, cv_ref, *, tm, seq):
    @pl.when(pl.program_id(1) == 0)
    def _():
        _fill_normed(h_ref, xp_ref, x_ref, xn_ref, g_ref[...], tm)
        o_ref[...] = x_ref[...]

    h = h_ref[...]
    cv = _dot(h, wc_ref[...]) * _dot(h, wv_ref[...])
    cv_ref[...] = jnp.where(_in_sequence(tm, seq), cv, 0.0)
    conv = _dwconv(cv_ref, taps_ref[...], tm)
    b = _dot(h_ref[pl.ds(HALO, tm), :], wb_ref[...])
    o_ref[...] += _dot((b * conv).astype(BF), wo_ref[...])


def _short_conv_mixer(x, g, w_in, taps, w_out, layer, j, *, tm, tn):
    seq, d = x.shape
    nd = d // tn
    kw = taps.shape[1]
    return pl.pallas_call(
        functools.partial(_sc_kernel, tm=tm, seq=seq),
        grid=(seq // tm, nd),
        in_specs=_halo_specs(tm, d, seq) + [
            pl.BlockSpec((None, 1, d), lambda i, c: (layer, 0, 0)),
            pl.BlockSpec((None, d, tn), lambda i, c: (j, 0, c)),
            pl.BlockSpec((None, d, tn), lambda i, c: (j, 0, nd + c)),
            pl.BlockSpec((None, d, tn), lambda i, c: (j, 0, 2 * nd + c)),
            pl.BlockSpec((None, kw, tn), lambda i, c: (j, 0, c)),
            pl.BlockSpec((None, tn, d), lambda i, c: (j, c, 0)),
        ],
        out_specs=pl.BlockSpec((tm, d), lambda i, c: (i, 0)),
        out_shape=jax.ShapeDtypeStruct((seq, d), F32),
        scratch_shapes=[pltpu.VMEM((tm + 2 * HALO, d), BF),
                        pltpu.VMEM((tm + 2 * HALO, tn), F32)],
        compiler_params=_params(48),
        name="short_conv",
    )(x, x, x, g, w_in, w_in, w_in, taps, w_out)


def _cf_kernel(xp_ref, x_ref, xn_ref, g_ref, wa_ref, wg_ref, ba_ref, bg_ref, taps_ref, bdw_ref,
               lng_ref, lnb_ref, w2_ref, b2_ref, o_ref, h_ref, a0_ref, a1_ref, shift_ref, conv_ref, *, tm, seq):
    c = pl.program_id(1)
    nd = pl.num_programs(1) - 1
    tw = a0_ref.shape[1]

    @pl.when(c == 0)
    def _():
        _fill_normed(h_ref, xp_ref, x_ref, xn_ref, g_ref[...], tm)

    @pl.when(jnp.logical_and(pl.program_id(0) == 0, c == 0))
    def _():
        a1_ref[...] = jnp.zeros_like(a1_ref)

    def step(proj_ref, conv_src_ref):
        _dwconv_wide(conv_src_ref, shift_ref, taps_ref, bdw_ref, conv_ref.at[jnp.maximum(c - 1, 0)], tm)
        h = h_ref[...]
        a = (_dot(h, wa_ref[...]) + ba_ref[...]) * jax.nn.sigmoid(_dot(h, wg_ref[...]) + bg_ref[...])
        proj_ref[...] = jnp.where(_in_sequence(tm, seq), a, 0.0)

    @pl.when(c % 2 == 0)
    def _():
        step(a0_ref, a1_ref)

    @pl.when(c % 2 == 1)
    def _():
        step(a1_ref, a0_ref)

    @pl.when(c == nd)
    def _():
        chunks = conv_ref.shape[0]
        d = chunks * tw
        total = jnp.zeros((tm, 1), F32)
        for k in range(chunks):
            total = total + jnp.sum(conv_ref[k], axis=-1, keepdims=True)
        mu = total / d
        sq = jnp.zeros((tm, 1), F32)
        for k in range(chunks):
            xc = conv_ref[k] - mu
            sq = sq + jnp.sum(xc * xc, axis=-1, keepdims=True)
        rstd = lax.rsqrt(sq / d + LN_EPS)
        o_ref[...] = x_ref[...] + b2_ref[...]
        for k in range(chunks):
            y = (conv_ref[k] - mu) * rstd * lng_ref[:, k * tw:(k + 1) * tw] + lnb_ref[:, k * tw:(k + 1) * tw]
            s = (y * jax.nn.sigmoid(y)).astype(BF)
            o_ref[...] += _dot(s, w2_ref[k * tw:(k + 1) * tw, :])


def _conformer_mixer(x, g, w1, b1, taps, bdw, lng, lnb, w2, b2, layer, j, *, tm, tn):
    seq, d = x.shape
    nd = d // tn
    kw = taps.shape[1]
    span = (HALO + kw // 2) // V7X_SUBLANES * V7X_SUBLANES + tm
    row = lambda i, c: (j, 0, 0)
    proj = lambda c: jnp.minimum(c, nd - 1)
    conv = lambda c: jnp.maximum(c - 1, 0)
    return pl.pallas_call(
        functools.partial(_cf_kernel, tm=tm, seq=seq),
        grid=(seq // tm, nd + 1),
        in_specs=_halo_specs(tm, d, seq) + [
            pl.BlockSpec((None, 1, d), lambda i, c: (layer, 0, 0)),
            pl.BlockSpec((None, d, tn), lambda i, c: (j, 0, proj(c))),
            pl.BlockSpec((None, d, tn), lambda i, c: (j, 0, nd + proj(c))),
            pl.BlockSpec((None, 1, tn), lambda i, c: (j, 0, proj(c))),
            pl.BlockSpec((None, 1, tn), lambda i, c: (j, 0, nd + proj(c))),
            pl.BlockSpec((None, kw, V7X_SUBLANES, tn), lambda i, c: (j, 0, 0, conv(c))),
            pl.BlockSpec((None, 1, tn), lambda i, c: (j, 0, conv(c))),
            pl.BlockSpec((None, 1, d), row),
            pl.BlockSpec((None, 1, d), row),
            _resident((None, d, d), row),
            pl.BlockSpec((None, 1, d), row),
        ],
        out_specs=pl.BlockSpec((tm, d), lambda i, c: (i, 0)),
        out_shape=jax.ShapeDtypeStruct((seq, d), F32),
        scratch_shapes=[pltpu.VMEM((tm + 2 * HALO, d), BF),
                        pltpu.VMEM((tm + 2 * HALO, tn), F32),
                        pltpu.VMEM((tm + 2 * HALO, tn), F32),
                        pltpu.VMEM((V7X_SUBLANES - 1, span, tn), F32),
                        pltpu.VMEM((nd, tm, tn), F32)],
        compiler_params=_params(48),
        name="conformer",
    )(x, x, x, g, w1, w1, b1, b1, taps, bdw, lng, lnb, w2, b2)


def _hy_in_kernel(xp_ref, x_ref, xn_ref, g_ref, w0_ref, w1_ref, w2_ref, t0_ref, t1_ref, t2_ref,
                  b0_ref, b1_ref, b2_ref, x0_ref, z_ref, h_ref, u_ref, *, tm, seq):
    @pl.when(pl.program_id(1) == 0)
    def _():
        _fill_normed(h_ref, xp_ref, x_ref, xn_ref, g_ref[...], tm)

    h = h_ref[...]
    inside = _in_sequence(tm, seq)

    def branch(w_ref, t_ref, b_ref):
        u_ref[...] = jnp.where(inside, _dot(h, w_ref[...]), 0.0)
        return _dwconv(u_ref, t_ref[...], tm) + b_ref[...]

    x0_ref[...] = branch(w0_ref, t0_ref, b0_ref).astype(BF)
    x1 = branch(w1_ref, t1_ref, b1_ref)
    z_ref[...] = (branch(w2_ref, t2_ref, b2_ref) * x1).astype(BF)


def _hyena_in(x, g, w_in, taps, bias, layer, j, *, tm, tn):
    seq, d = x.shape
    nd = d // tn
    kw = taps.shape[1]
    col = lambda k: (lambda i, c: (j, 0, k * nd + c))
    return pl.pallas_call(
        functools.partial(_hy_in_kernel, tm=tm, seq=seq),
        grid=(seq // tm, nd),
        in_specs=_halo_specs(tm, d, seq)
        + [pl.BlockSpec((None, 1, d), lambda i, c: (layer, 0, 0))]
        + [pl.BlockSpec((None, d, tn), col(k)) for k in range(3)]
        + [pl.BlockSpec((None, kw, tn), col(k)) for k in range(3)]
        + [pl.BlockSpec((None, 1, tn), col(k)) for k in range(3)],
        out_specs=[pl.BlockSpec((tm, tn), lambda i, c: (i, c))] * 2,
        out_shape=[jax.ShapeDtypeStruct((seq, d), BF)] * 2,
        scratch_shapes=[pltpu.VMEM((tm + 2 * HALO, d), BF),
                        pltpu.VMEM((tm + 2 * HALO, tn), F32)],
        compiler_params=_params(40),
        name="hyena_in",
    )(x, x, x, g, w_in, w_in, w_in, taps, taps, taps, bias, bias, bias)


GROUP = V7X_BF16_SUBLANES


def _dft_plan(seq):
    n = 2 * seq
    n2 = 256 if n >= 256 * 32 else 64
    n1 = n // n2
    h1 = n1 // 2
    assert h1 & (h1 - 1) == 0 and h1 >= V7X_BF16_SUBLANES, "sequence length must be a power of two"
    k1p = _round_up(h1 + 1, V7X_BF16_SUBLANES)
    pack = max(1, V7X_LANES // h1)
    return n, n1, n2, h1, k1p, pack


def _dft_tables(seq):
    n, n1, n2, h1, k1p, pack = _dft_plan(seq)
    khalf = _round_up(k1p, V7X_LANES)
    two_pi = 2.0 * math.pi

    k1 = np.arange(k1p)
    pos = np.arange(h1)[None, :] * n2 + np.arange(n2)[:, None]
    ang = ((k1[None, :, None] * pos[:, None, :]) % n) * (two_pi / n)
    fwd = np.concatenate([np.cos(ang), -np.sin(ang)], axis=1)
    fwd = fwd.reshape(n2 // pack, pack, 2 * k1p, 1, h1) * np.eye(pack)[None, :, None, :, None]
    fwd = fwd.reshape(n2 // pack, pack * 2 * k1p, pack * h1)

    weight = np.where(k1 == 0, 1.0, np.where(k1 < n1 // 2, 2.0, np.where(k1 == n1 // 2, 1.0, 0.0))) / n
    angt = np.swapaxes(ang, 1, 2)
    widen = ((0, 0), (0, 0), (0, khalf - k1p))
    inv = np.concatenate([np.pad(np.cos(angt) * weight, widen),
                          np.pad(-np.sin(angt) * weight, widen)], axis=2)

    idx = np.arange(n2)
    a2 = ((idx[:, None] * idx[None, :]) % n2) * (two_pi / n2)
    cr, ci = np.cos(a2), -np.sin(a2)
    mid_fwd = np.block([[cr, -ci], [ci, cr]])
    mid_inv = np.block([[cr, ci], [-ci, cr]])
    return tuple(jnp.asarray(t.astype(BF)) for t in (fwd, inv, mid_fwd, mid_inv))


def _group_positions(step, h1, n2, first=0, rows=None):
    rows = GROUP * h1 if rows is None else rows
    e = first + lax.broadcasted_iota(jnp.int32, (rows, 1), 0)
    q = lax.shift_right_logical(e, jnp.int32(int(math.log2(h1))))
    return (e - q * h1) * n2 + step * GROUP + q


def _phase_tables(seq, bands, width):
    n, n1, n2, h1, k1p, pack = _dft_plan(seq)
    f = np.linspace(1e-4, bands - 1, bands, dtype=np.float32).astype(np.float64)
    lanes = np.zeros((width,))
    lanes[1:bands + 1] = f
    lanes[bands + 1:2 * bands + 1] = f
    major = (2.0 * np.pi / seq) * (np.arange(h1) * n2)[:, None] * lanes[None, :]
    minor = (2.0 * np.pi / seq) * np.arange(n2)[:, None] * lanes[None, :]
    tile = lambda a: np.broadcast_to(a[:, None, :], (n2, V7X_SUBLANES, width))
    tabs = (np.cos(major), np.sin(major), tile(np.cos(minor)), tile(np.sin(minor)))
    return tuple(jnp.asarray(np.ascontiguousarray(a), dtype=F32) for a in tabs)


def _hy_mlp_kernel(ca_ref, sa_ref, cb_ref, sb_ref, w1_ref, b1_ref, w2_ref, b2_ref, w3_ref, b3_ref, fr_ref, a_ref,
                   *, h1, n2, seq, bands):
    rows, width = GROUP * h1, ca_ref.shape[1]
    t = _group_positions(pl.program_id(0), h1, n2).astype(F32) / (seq - 1)
    split = lambda r: r[...].reshape(1, h1 // V7X_SUBLANES, V7X_SUBLANES, width)
    ca, sa = split(ca_ref), split(sa_ref)
    cb, sb = cb_ref[...][:, None], sb_ref[...][:, None]
    cos = (ca * cb - sa * sb).reshape(rows, width)
    sin = (sa * cb + ca * sb).reshape(rows, width)
    lane = lax.broadcasted_iota(jnp.int32, (1, width), 1)
    feat = jnp.where(lane == 0, t,
                     jnp.where(lane <= bands, cos, jnp.where(lane <= 2 * bands, -sin, 0.0)))
    fr = fr_ref[...]
    a = jnp.sin(fr * (_dot_hi(feat, w1_ref[...]) + b1_ref[...]))
    a = jnp.sin(fr * (_dot_hi(a, w2_ref[...]) + b2_ref[...]))
    a = jnp.sin(fr * (_dot_hi(a, w3_ref[...]) + b3_ref[...]))
    a_ref[...] = a.astype(BF)


def _hyena_filter_features(seq, w1, b1, w2, b2, w3, b3, fr, bands):
    n, n1, n2, h1, k1p, pack = _dft_plan(seq)
    wp = w1.shape[1]
    ca, sa, cb, sb = _phase_tables(seq, bands, w1.shape[0])
    full = lambda a: pl.BlockSpec(a.shape, lambda s: (0,) * a.ndim)
    group = pl.BlockSpec((GROUP,) + cb.shape[1:], lambda s: (s, 0, 0))
    args = (ca, sa, cb, sb, w1, b1, w2, b2, w3, b3, fr)
    return pl.pallas_call(
        functools.partial(_hy_mlp_kernel, h1=h1, n2=n2, seq=seq, bands=bands),
        grid=(n2 // GROUP,),
        in_specs=[full(ca), full(sa), group, group] + [full(a) for a in args[4:]],
        out_specs=pl.BlockSpec((GROUP * h1, wp), lambda s: (s, 0)),
        out_shape=jax.ShapeDtypeStruct((seq, wp), BF),
        compiler_params=_params(32),
        name="hyena_filter_mlp",
    )(*args)


def _dft_a_kernel(tab_ref, z_ref, a3_ref, w4f_ref, w4b_ref, dl_ref, az_ref, af_ref, ab_ref, ssq_ref,
                  zt_ref, stage_ref, *, h1, n2, k1p, pack, seq):
    step = pl.program_id(1)
    cw = z_ref.shape[-1]
    kin = pack * h1

    @pl.when(step == 0)
    def _():
        ssq_ref[...] = jnp.zeros_like(ssq_ref)

    zt_ref[...] = pltpu.einshape("abc->bac", z_ref[...]).reshape(GROUP * h1, cw)
    ssq = jnp.zeros((1, cw), F32)
    for p in range(GROUP // pack):
        rows = pl.ds(p * kin, kin)
        pos = _group_positions(step, h1, n2, p * kin, kin)
        decay = jnp.exp(-(pos.astype(F32) / (seq - 1)) * dl_ref[...])
        a3 = a3_ref[rows, :]
        hf = _dot(a3, w4f_ref[...]) * decay
        hb = jnp.where(pos == 0, 0.0, _dot(a3, w4b_ref[...]) * decay)
        ssq = ssq + jnp.sum(hf * hf + hb * hb, axis=0, keepdims=True)
        tab = tab_ref[p]
        for s, src in enumerate((zt_ref[rows, :], hf.astype(BF), hb.astype(BF))):
            a = _dot(tab, src)
            stage_ref[s, pl.ds(p * pack, pack)] = a.astype(BF).reshape(pack, 2 * k1p, cw)
    ssq_ref[...] += ssq
    for s, dst in enumerate((az_ref, af_ref, ab_ref)):
        dst[...] = pltpu.einshape("qkc->kqc", stage_ref[s])


def _dft_stage_a(tab, z, a3, w4, deltas, seq, d, *, cw):
    n, n1, n2, h1, k1p, pack = _dft_plan(seq)
    wp = a3.shape[1]
    nc = d // cw
    spectrum = pl.BlockSpec((2 * k1p, GROUP, cw), lambda c, s: (0, s, c))
    shape = jax.ShapeDtypeStruct((2 * k1p, n2, d), BF)
    return pl.pallas_call(
        functools.partial(_dft_a_kernel, h1=h1, n2=n2, k1p=k1p, pack=pack, seq=seq),
        grid=(nc, n2 // GROUP),
        in_specs=[pl.BlockSpec((GROUP // pack,) + tab.shape[1:], lambda c, s: (s, 0, 0)),
                  pl.BlockSpec((h1, GROUP, cw), lambda c, s: (0, s, c)),
                  pl.BlockSpec((GROUP * h1, wp), lambda c, s: (s, 0)),
                  pl.BlockSpec((wp, cw), lambda c, s: (0, c)),
                  pl.BlockSpec((wp, cw), lambda c, s: (0, nc + c)),
                  pl.BlockSpec((1, cw), lambda c, s: (0, c))],
        out_specs=[spectrum] * 3 + [pl.BlockSpec((1, cw), lambda c, s: (0, c))],
        out_shape=[shape] * 3 + [jax.ShapeDtypeStruct((1, d), F32)],
        scratch_shapes=[pltpu.VMEM((GROUP * h1, cw), BF),
                        pltpu.VMEM((3, GROUP, 2 * k1p, cw), BF)],
        compiler_params=_params(48),
        name="dft_stage_a",
    )(tab, z.reshape(h1, n2, d), a3, w4, w4, deltas)


def _dft_b_kernel(mf_ref, mi_ref, az_ref, af_ref, ab_ref, s_ref, *, n2, d, cw, nyquist):
    @pl.when(pl.program_id(0) > nyquist)
    def _():
        s_ref[...] = jnp.zeros_like(s_ref)

    @pl.when(pl.program_id(0) <= nyquist)
    def _():
        mf = mf_ref[...]
        mi = mi_ref[...]
        def forward(c):
            cols = pl.ds(c * cw, cw)
            return [_dot(mf, r[:, 0, :, cols].reshape(2 * n2, cw)) for r in (az_ref, af_ref, ab_ref)]

        ahead = forward(0)
        for c in range(d // cw):
            cols = pl.ds(c * cw, cw)
            xz, xf, xb = ahead
            if c + 1 < d // cw:
                ahead = forward(c + 1)
            kr = xf[:n2] + xb[:n2]
            ki = xf[n2:] - xb[n2:]
            yr = xz[:n2] * kr - xz[n2:] * ki
            yi = xz[:n2] * ki + xz[n2:] * kr
            s = _dot(mi, jnp.concatenate([yr, yi], axis=0).astype(BF))
            s_ref[0, 0, :, cols] = s[:n2].astype(BF)
            s_ref[1, 0, :, cols] = s[n2:].astype(BF)


def _dft_stage_b(mid_fwd, mid_inv, az, af, ab, seq, d, *, cw):
    n, n1, n2, h1, k1p, pack = _dft_plan(seq)
    view = lambda a: a.reshape(2, k1p, n2, d)
    blk = pl.BlockSpec((2, 1, n2, d), lambda k: (0, k, 0, 0))
    src = pl.BlockSpec((2, 1, n2, d), lambda k: (0, jnp.minimum(k, h1), 0, 0))
    mat = _resident((2 * n2, 2 * n2), lambda k: (0, 0))
    return pl.pallas_call(
        functools.partial(_dft_b_kernel, n2=n2, d=d, cw=cw, nyquist=h1),
        grid=(k1p,),
        in_specs=[mat, mat, src, src, src],
        out_specs=blk,
        out_shape=jax.ShapeDtypeStruct((2, k1p, n2, d), BF),
        compiler_params=_params(48),
        name="dft_stage_b",
    )(mid_fwd, mid_inv, view(az), view(af), view(ab))


def _dft_c_kernel(tab_ref, s_ref, y_ref, pad_ref, st_ref, ys_ref, *, k1p):
    khalf = pad_ref.shape[0] // 2

    @pl.when(jnp.logical_and(pl.program_id(0) == 0, pl.program_id(1) == 0))
    def _():
        pad_ref[...] = jnp.zeros_like(pad_ref)

    st_ref[...] = pltpu.einshape("kqc->qkc", s_ref[...])
    for q in range(GROUP):
        pad_ref[pl.ds(0, k1p), :] = st_ref[q, pl.ds(0, k1p), :]
        pad_ref[pl.ds(khalf, k1p), :] = st_ref[q, pl.ds(k1p, k1p), :]
        ys_ref[q] = _dot(tab_ref[q], pad_ref[...]).astype(BF)
    y_ref[...] = pltpu.einshape("qnc->nqc", ys_ref[...])


def _dft_stage_c(tab, s, seq, d, *, cw):
    n, n1, n2, h1, k1p, pack = _dft_plan(seq)
    kc = tab.shape[2]
    y = pl.pallas_call(
        functools.partial(_dft_c_kernel, k1p=k1p),
        grid=(d // cw, n2 // GROUP),
        in_specs=[pl.BlockSpec((GROUP, h1, kc), lambda c, q: (q, 0, 0)),
                  pl.BlockSpec((2 * k1p, GROUP, cw), lambda c, q: (0, q, c))],
        out_specs=pl.BlockSpec((h1, GROUP, cw), lambda c, q: (0, q, c)),
        out_shape=jax.ShapeDtypeStruct((h1, n2, d), BF),
        scratch_shapes=[pltpu.VMEM((kc, cw), BF),
                        pltpu.VMEM((GROUP, 2 * k1p, cw), BF),
                        pltpu.VMEM((GROUP, h1, cw), BF)],
        compiler_params=_params(40),
        name="dft_stage_c",
    )(tab, s.reshape(2 * k1p, n2, d))
    return y.reshape(seq, d)


def _hy_out_kernel(x_ref, y_ref, z_ref, x0_ref, ssq_ref, skip_ref, wo_ref, o_ref):
    y = y_ref[...].astype(F32) * lax.rsqrt(ssq_ref[...]) + z_ref[...].astype(F32) * skip_ref[...]
    gated = (y * x0_ref[...].astype(F32)).astype(BF)
    o_ref[...] = x_ref[...] + _dot(gated, wo_ref[...])


def _hyena_out(x, y, z, x0, ssq, skip, w_out, j, *, tm):
    seq, d = x.shape
    tile = pl.BlockSpec((tm, d), lambda i: (i, 0))
    return pl.pallas_call(
        _hy_out_kernel,
        grid=(seq // tm,),
        in_specs=[tile, tile, tile, tile,
                  pl.BlockSpec((1, d), lambda i: (0, 0)),
                  pl.BlockSpec((None, 1, d), lambda i: (j, 0, 0)),
                  _resident((None, d, d), lambda i: (j, 0, 0))],
        out_specs=tile,
        out_shape=jax.ShapeDtypeStruct((seq, d), F32),
        compiler_params=_params(40),
        name="hyena_out",
    )(x, y, z, x0, ssq, skip, w_out)


def _hyena_mixer(x, g, w_in, taps, bias, f_w1, f_b1, f_w2, f_b2, f_w3, f_b3, f_w4, freq, skip, w_out,
                 layer, j, *, tm, tn):
    seq, d = x.shape
    x0, z = _hyena_in(x, g, w_in, taps, bias, layer, j, tm=tm, tn=tn)

    emb, order = f_w1.shape[1], f_w1.shape[2]
    bands = (emb - 1) // 2
    wp = _round_up(order, V7X_LANES)
    ep = _round_up(emb, V7X_LANES)
    pad2 = lambda a, r, c: jnp.pad(a.astype(F32), ((0, r - a.shape[0]), (0, c - a.shape[1])))
    deltas = jnp.abs(jnp.linspace(math.log(HY_TARGET) / HY_FAST_DECAY_PCT,
                                  math.log(HY_TARGET) / HY_SLOW_DECAY_PCT, d, dtype=F32))[None, :]
    a3 = _hyena_filter_features(
        seq,
        pad2(f_w1[j], ep, wp), pad2(f_b1[j][None, :], 1, wp),
        pad2(f_w2[j], wp, wp), pad2(f_b2[j][None, :], 1, wp),
        pad2(f_w3[j], wp, wp), pad2(f_b3[j][None, :], 1, wp),
        pad2(freq[j][None, :], 1, wp), bands)

    cw = min(512, d)
    tab_a, tab_c, mid_fwd, mid_inv = _dft_tables(seq)
    az, af, ab, ssq = _dft_stage_a(tab_a, z, a3, pad2(f_w4[j], wp, 2 * d).astype(BF), deltas, seq, d, cw=cw)
    s = _dft_stage_b(mid_fwd, mid_inv, az, af, ab, seq, d, cw=cw)
    y = _dft_stage_c(tab_c, s, seq, d, cw=cw)
    return _hyena_out(x, y, z, x0, ssq, skip, w_out, j, tm=min(256, seq))


def kernel(x, p, norm_ffn1, ffn1_w_gate, ffn1_w_up, ffn1_w_down, norm_mix, norm_ffn2, ffn2_w_gate, ffn2_w_up, ffn2_w_down, norm_ple, ple_w_gate, ple_w_up, sc_w_in, sc_conv, sc_w_out, cf_w_pw1, cf_b_pw1, cf_dw, cf_b_dw, cf_ln_g, cf_ln_b, cf_w_pw2, cf_b_pw2, hy_w_in, hy_short, hy_b_short, hy_f_w1, hy_f_b1, hy_f_w2, hy_f_b2, hy_f_w3, hy_f_b3, hy_f_w4, hy_freq, hy_skip, hy_w_out, norm_final):
    batch, seq, d = x.shape
    depth = p.shape[0]
    assert batch == 1, "the row tiling treats the sequence as the only token axis"
    tm = min(512, seq)
    tn = min(512, d)
    tn_cf = min(256, d)
    tf = min(512, ffn1_w_gate.shape[-1])
    tm_ffn = min(1024, seq)

    bf = lambda a: a.astype(BF)
    row = lambda a: a[:, None, :].astype(F32)
    ffn1_w = (ffn1_w_gate, ffn1_w_up, ffn1_w_down)
    ffn2_w = (ffn2_w_gate, ffn2_w_up, ffn2_w_down)
    w_cur = tuple(bf(w[0]) for w in ffn1_w)
    n_ffn1, n_mix, n_ffn2, n_ple = row(norm_ffn1), row(norm_mix), row(norm_ffn2), row(norm_ple)
    ple_gate, ple_up = bf(ple_w_gate), bf(ple_w_up)
    sc_in, sc_out = bf(sc_w_in), bf(sc_w_out)
    cf_w1, cf_w2 = bf(cf_w_pw1), bf(cf_w_pw2)
    hy_in, hy_out = bf(hy_w_in), bf(hy_w_out)
    gf = norm_final[None, :].astype(F32)

    xs = x.reshape(seq, d)
    ps = p.reshape(depth, seq, p.shape[-1])
    for i in range(depth):
        m, j = i % 3, i // 3
        xs, w_nxt = _ffn(xs, n_ffn1, *w_cur, i, ffn2_w + (i,), tm=tm_ffn, tf=tf)
        if m == 0:
            xs = _short_conv_mixer(xs, n_mix, sc_in, sc_conv.astype(F32), sc_out, i, j, tm=tm, tn=tn)
        elif m == 1:
            cf_taps = jnp.broadcast_to(cf_dw.astype(F32)[:, :, None, :], cf_dw.shape[:2] + (V7X_SUBLANES, d))
            xs = _conformer_mixer(xs, n_mix, cf_w1, row(cf_b_pw1), cf_taps, row(cf_b_dw),
                                  row(cf_ln_g), row(cf_ln_b), cf_w2, row(cf_b_pw2), i, j, tm=tm, tn=tn_cf)
        else:
            xs = _hyena_mixer(xs, n_mix, hy_in, hy_short.astype(F32), row(hy_b_short),
                              hy_f_w1, hy_f_b1, hy_f_w2, hy_f_b2, hy_f_w3, hy_f_b3, hy_f_w4,
                              hy_freq, row(hy_skip), hy_out, i, j, tm=tm, tn=tn)
        xs, w_cur = _ffn(xs, n_ffn2, *w_nxt, i, ffn1_w + (i + 1,) if i + 1 < depth else None, tm=tm_ffn, tf=tf)
        xs = _ple(xs, ps, n_ple, ple_gate, ple_up, gf, i, tm=tm, final=(i == depth - 1))
    return xs.reshape(batch, seq, d)
```

```python
import functools
import math

import jax
import jax.numpy as jnp
import numpy as np
from jax import lax
from jax.experimental import pallas as pl
from jax.experimental.pallas import tpu as pltpu

BF = jnp.bfloat16
F32 = jnp.float32

RMS_EPS = 1e-6
LN_EPS = 1e-5
HY_FAST_DECAY_PCT = 0.3
HY_SLOW_DECAY_PCT = 1.5
HY_TARGET = 1e-2

V7X_LANES = 128
V7X_SUBLANES = 8
V7X_BF16_SUBLANES = 16
HALO = V7X_BF16_SUBLANES
FFN_X_PARTS = 4
MIB = 1024 * 1024


def _dot(a, b):
    return jnp.dot(a, b, preferred_element_type=F32)


def _dot_hi(a, b):
    return jnp.dot(a, b, preferred_element_type=F32, precision=lax.Precision.HIGHEST)


def _rms(x, g):
    return x * lax.rsqrt(jnp.mean(x * x, axis=-1, keepdims=True) + RMS_EPS) * g


def _round_up(n, m):
    return (n + m - 1) // m * m


def _params(vmem_mib):
    return pltpu.CompilerParams(vmem_limit_bytes=int(vmem_mib * MIB))


def _resident(shape, index_map):
    return pl.BlockSpec(shape, index_map, pipeline_mode=pl.Buffered(1))


def _ffn_kernel(x_ref, g_ref, wg_ref, wu_ref, wd_ref, *rest, cast_blocks):
    if cast_blocks is None:
        o_ref, h_ref = rest
    else:
        ng_ref, nu_ref, nd_ref, o_ref, og_ref, ou_ref, od_ref, h_ref = rest
        step = pl.program_id(0) * pl.num_programs(1) + pl.program_id(1)

        @pl.when(step < cast_blocks[0])
        def _():
            og_ref[...] = ng_ref[...].astype(BF)
            ou_ref[...] = nu_ref[...].astype(BF)

        @pl.when(step < cast_blocks[1])
        def _():
            od_ref[...] = nd_ref[...].astype(BF)

    j = pl.program_id(1)
    rows = x_ref.shape[0]
    for part in range(FFN_X_PARTS):
        @pl.when(j == part)
        def _():
            x = x_ref[...]
            h_ref[pl.ds(part * rows, rows), :] = _rms(x, g_ref[...]).astype(BF)
            o_ref[pl.ds(part * rows, rows), :] = x

    @pl.when(j >= FFN_X_PARTS - 1)
    def _():
        h = h_ref[...]
        a = _dot(h, wg_ref[...])
        u = _dot(h, wu_ref[...])
        act = (a * jax.nn.sigmoid(a) * u * 0.5).astype(BF)
        o_ref[...] += _dot(act, wd_ref[...])


def _cast_rows(rows, steps):
    r = _round_up(-(-rows // steps), V7X_BF16_SUBLANES)
    while rows % r:
        r += V7X_BF16_SUBLANES
    return r


def _ffn(x, g, wg, wu, wd, layer, nxt, *, tm, tf):
    seq, d = x.shape
    f = wg.shape[-1]
    lead = FFN_X_PARTS - 1
    nj = f // tf + lead
    grid = (seq // tm, nj)
    chunk = lambda j: jnp.maximum(j - lead, 0)
    in_specs = [
        pl.BlockSpec((tm // FFN_X_PARTS, d), lambda i, j: (FFN_X_PARTS * i + jnp.minimum(j, lead), 0)),
        pl.BlockSpec((None, 1, d), lambda i, j: (layer, 0, 0)),
        pl.BlockSpec((d, tf), lambda i, j: (0, chunk(j))),
        pl.BlockSpec((d, tf), lambda i, j: (0, chunk(j))),
        pl.BlockSpec((tf, d), lambda i, j: (chunk(j), 0)),
    ]
    out_specs = [pl.BlockSpec((tm, d), lambda i, j: (i, 0))]
    out_shape = [jax.ShapeDtypeStruct((seq, d), F32)]
    args = [x, g, wg, wu, wd]
    cast_blocks = None
    if nxt is not None:
        ng, nu, nd, nl = nxt
        steps = grid[0] * grid[1]
        rg, rd = _cast_rows(d, steps), _cast_rows(f, steps)
        cast_blocks = (d // rg, f // rd)
        blk = lambda n: (lambda i, j: (jnp.minimum(i * nj + j, n - 1), 0))
        blk3 = lambda n: (lambda i, j: (nl, jnp.minimum(i * nj + j, n - 1), 0))
        in_specs += [pl.BlockSpec((None, rg, f), blk3(cast_blocks[0])),
                     pl.BlockSpec((None, rg, f), blk3(cast_blocks[0])),
                     pl.BlockSpec((None, rd, d), blk3(cast_blocks[1]))]
        out_specs += [pl.BlockSpec((rg, f), blk(cast_blocks[0])),
                      pl.BlockSpec((rg, f), blk(cast_blocks[0])),
                      pl.BlockSpec((rd, d), blk(cast_blocks[1]))]
        out_shape += [jax.ShapeDtypeStruct((d, f), BF), jax.ShapeDtypeStruct((d, f), BF),
                      jax.ShapeDtypeStruct((f, d), BF)]
        args += [ng, nu, nd]
    outs = pl.pallas_call(
        functools.partial(_ffn_kernel, cast_blocks=cast_blocks),
        grid=grid,
        in_specs=in_specs,
        out_specs=out_specs,
        out_shape=out_shape,
        scratch_shapes=[pltpu.VMEM((tm, d), BF)],
        compiler_params=_params(48),
        name="ffn",
    )(*args)
    return outs[0], (tuple(outs[1:]) if nxt is not None else None)


def _ple_kernel(x_ref, p_ref, g_ref, wgate_ref, wup_ref, gf_ref, o_ref, *, final):
    x = x_ref[...]
    h = _rms(x, g_ref[...]).astype(BF)
    gate = jax.nn.sigmoid(_dot(h, wgate_ref[...]))
    up = _dot(p_ref[...].astype(BF), wup_ref[...])
    y = x + up * gate
    if final:
        y = _rms(y, gf_ref[...])
    o_ref[...] = y


def _ple(x, p, g, wgate, wup, gf, layer, *, tm, final):
    seq, d = x.shape
    pd = p.shape[-1]
    return pl.pallas_call(
        functools.partial(_ple_kernel, final=final),
        grid=(seq // tm,),
        in_specs=[
            pl.BlockSpec((tm, d), lambda i: (i, 0)),
            pl.BlockSpec((None, tm, pd), lambda i: (layer, i, 0)),
            pl.BlockSpec((None, 1, d), lambda i: (layer, 0, 0)),
            _resident((None, d, d), lambda i: (layer, 0, 0)),
            _resident((None, pd, d), lambda i: (layer, 0, 0)),
            pl.BlockSpec((1, d), lambda i: (0, 0)),
        ],
        out_specs=pl.BlockSpec((tm, d), lambda i: (i, 0)),
        out_shape=jax.ShapeDtypeStruct((seq, d), F32),
        compiler_params=_params(48),
        name="ple",
    )(x, p, g, wgate, wup, gf)


def _halo_specs(tm, d, seq):
    per = tm // HALO
    last = seq // HALO - 1
    return [
        pl.BlockSpec((HALO, d), lambda i, j: (jnp.maximum(i * per - 1, 0), 0)),
        pl.BlockSpec((tm, d), lambda i, j: (i, 0)),
        pl.BlockSpec((HALO, d), lambda i, j: (jnp.minimum((i + 1) * per, last), 0)),
    ]


def _fill_normed(h_ref, xp_ref, x_ref, xn_ref, g, tm):
    h_ref[pl.ds(0, HALO), :] = _rms(xp_ref[...], g).astype(BF)
    h_ref[pl.ds(HALO, tm), :] = _rms(x_ref[...], g).astype(BF)
    h_ref[pl.ds(HALO + tm, HALO), :] = _rms(xn_ref[...], g).astype(BF)


def _in_sequence(tm, seq):
    row = pl.program_id(0) * tm - HALO + lax.broadcasted_iota(jnp.int32, (tm + 2 * HALO, 1), 0)
    return jnp.logical_and(row >= 0, row < seq)


def _dwconv(src_ref, taps, rows, start=0):
    k = taps.shape[0]
    first = HALO - k // 2 + start
    acc = taps[0:1] * src_ref[pl.ds(first, rows), :]
    for t in range(1, k):
        acc = acc + taps[t:t + 1] * src_ref[pl.ds(first + t, rows), :]
    return acc


CONV_ROWS = 32


def _dwconv_wide(src_ref, shift_ref, taps_ref, bias_ref, dst_ref, tm):
    k = taps_ref.shape[0]
    first = HALO - k // 2
    span = shift_ref.shape[1]
    ch = src_ref.shape[1]
    tiles = CONV_ROWS // V7X_SUBLANES
    for r in range(1, V7X_SUBLANES):
        shift_ref[r - 1] = src_ref[pl.ds(r, span), :]
    for blk in range(tm // CONV_ROWS):
        acc = None
        for t in range(k):
            base, r = divmod(first + t, V7X_SUBLANES)
            rows = pl.ds(base * V7X_SUBLANES + blk * CONV_ROWS, CONV_ROWS)
            src = src_ref[rows, :] if r == 0 else shift_ref[r - 1, rows, :]
            term = taps_ref[t][None] * src.reshape(tiles, V7X_SUBLANES, ch)
            acc = term if acc is None else acc + term
        dst_ref[pl.ds(blk * CONV_ROWS, CONV_ROWS), :] = acc.reshape(CONV_ROWS, ch) + bias_ref[...]


def _sc_kernel(xp_ref, x_ref, xn_ref, g_ref, wb_ref, wc_ref, wv_ref, taps_ref, wo_ref,
               o_ref, h_ref, cv_ref, *, tm, seq):
    @pl.when(pl.program_id(1) == 0)
    def _():
        _fill_normed(h_ref, xp_ref, x_ref, xn_ref, g_ref[...], tm)
        o_ref[...] = x_ref[...]

    h = h_ref[...]
    cv = _dot(h, wc_ref[...]) * _dot(h, wv_ref[...])
    cv_ref[...] = jnp.where(_in_sequence(tm, seq), cv, 0.0)
    conv = _dwconv(cv_ref, taps_ref[...], tm)
    b = _dot(h_ref[pl.ds(HALO, tm), :], wb_ref[...])
    o_ref[...] += _dot((b * conv).astype(BF), wo_ref[...])


def _short_conv_mixer(x, g, w_in, taps, w_out, layer, j, *, tm, tn):
    seq, d = x.shape
    nd = d // tn
    kw = taps.shape[1]
    return pl.pallas_call(
        functools.partial(_sc_kernel, tm=tm, seq=seq),
        grid=(seq // tm, nd),
        in_specs=_halo_specs(tm, d, seq) + [
            pl.BlockSpec((None, 1, d), lambda i, c: (layer, 0, 0)),
            pl.BlockSpec((None, d, tn), lambda i, c: (j, 0, c)),
            pl.BlockSpec((None, d, tn), lambda i, c: (j, 0, nd + c)),
            pl.BlockSpec((None, d, tn), lambda i, c: (j, 0, 2 * nd + c)),
            pl.BlockSpec((None, kw, tn), lambda i, c: (j, 0, c)),
            pl.BlockSpec((None, tn, d), lambda i, c: (j, c, 0)),
        ],
        out_specs=pl.BlockSpec((tm, d), lambda i, c: (i, 0)),
        out_shape=jax.ShapeDtypeStruct((seq, d), F32),
        scratch_shapes=[pltpu.VMEM((tm + 2 * HALO, d), BF),
                        pltpu.VMEM((tm + 2 * HALO, tn), F32)],
        compiler_params=_params(48),
        name="short_conv",
    )(x, x, x, g, w_in, w_in, w_in, taps, w_out)


def _cf_kernel(xp_ref, x_ref, xn_ref, g_ref, wa_ref, wg_ref, ba_ref, bg_ref, taps_ref, bdw_ref,
               lng_ref, lnb_ref, w2_ref, b2_ref, o_ref, h_ref, a0_ref, a1_ref, shift_ref, conv_ref, *, tm, seq):
    c = pl.program_id(1)
    nd = pl.num_programs(1) - 1
    tw = a0_ref.shape[1]

    @pl.when(c == 0)
    def _():
        _fill_normed(h_ref, xp_ref, x_ref, xn_ref, g_ref[...], tm)

    @pl.when(jnp.logical_and(pl.program_id(0) == 0, c == 0))
    def _():
        a1_ref[...] = jnp.zeros_like(a1_ref)

    def step(proj_ref, conv_src_ref):
        _dwconv_wide(conv_src_ref, shift_ref, taps_ref, bdw_ref, conv_ref.at[jnp.maximum(c - 1, 0)], tm)
        h = h_ref[...]
        a = (_dot(h, wa_ref[...]) + ba_ref[...]) * jax.nn.sigmoid(_dot(h, wg_ref[...]) + bg_ref[...])
        proj_ref[...] = jnp.where(_in_sequence(tm, seq), a, 0.0)

    @pl.when(c % 2 == 0)
    def _():
        step(a0_ref, a1_ref)

    @pl.when(c % 2 == 1)
    def _():
        step(a1_ref, a0_ref)

    @pl.when(c == nd)
    def _():
        chunks = conv_ref.shape[0]
        d = chunks * tw
        total = jnp.zeros((tm, 1), F32)
        for k in range(chunks):
            total = total + jnp.sum(conv_ref[k], axis=-1, keepdims=True)
        mu = total / d
        sq = jnp.zeros((tm, 1), F32)
        for k in range(chunks):
            xc = conv_ref[k] - mu
            sq = sq + jnp.sum(xc * xc, axis=-1, keepdims=True)
        rstd = lax.rsqrt(sq / d + LN_EPS)
        o_ref[...] = x_ref[...] + b2_ref[...]
        for k in range(chunks):
            y = (conv_ref[k] - mu) * rstd * lng_ref[:, k * tw:(k + 1) * tw] + lnb_ref[:, k * tw:(k + 1) * tw]
            s = (y * jax.nn.sigmoid(y)).astype(BF)
            o_ref[...] += _dot(s, w2_ref[k * tw:(k + 1) * tw, :])


def _conformer_mixer(x, g, w1, b1, taps, bdw, lng, lnb, w2, b2, layer, j, *, tm, tn):
    seq, d = x.shape
    nd = d // tn
    kw = taps.shape[1]
    span = (HALO + kw // 2) // V7X_SUBLANES * V7X_SUBLANES + tm
    row = lambda i, c: (j, 0, 0)
    proj = lambda c: jnp.minimum(c, nd - 1)
    conv = lambda c: jnp.maximum(c - 1, 0)
    return pl.pallas_call(
        functools.partial(_cf_kernel, tm=tm, seq=seq),
        grid=(seq // tm, nd + 1),
        in_specs=_halo_specs(tm, d, seq) + [
            pl.BlockSpec((None, 1, d), lambda i, c: (layer, 0, 0)),
            pl.BlockSpec((None, d, tn), lambda i, c: (j, 0, proj(c))),
            pl.BlockSpec((None, d, tn), lambda i, c: (j, 0, nd + proj(c))),
            pl.BlockSpec((None, 1, tn), lambda i, c: (j, 0, proj(c))),
            pl.BlockSpec((None, 1, tn), lambda i, c: (j, 0, nd + proj(c))),
            pl.BlockSpec((None, kw, V7X_SUBLANES, tn), lambda i, c: (j, 0, 0, conv(c))),
            pl.BlockSpec((None, 1, tn), lambda i, c: (j, 0, conv(c))),
            pl.BlockSpec((None, 1, d), row),
            pl.BlockSpec((None, 1, d), row),
            _resident((None, d, d), row),
            pl.BlockSpec((None, 1, d), row),
        ],
        out_specs=pl.BlockSpec((tm, d), lambda i, c: (i, 0)),
        out_shape=jax.ShapeDtypeStruct((seq, d), F32),
        scratch_shapes=[pltpu.VMEM((tm + 2 * HALO, d), BF),
                        pltpu.VMEM((tm + 2 * HALO, tn), F32),
                        pltpu.VMEM((tm + 2 * HALO, tn), F32),
                        pltpu.VMEM((V7X_SUBLANES - 1, span, tn), F32),
                        pltpu.VMEM((nd, tm, tn), F32)],
        compiler_params=_params(48),
        name="conformer",
    )(x, x, x, g, w1, w1, b1, b1, taps, bdw, lng, lnb, w2, b2)


def _hy_in_kernel(xp_ref, x_ref, xn_ref, g_ref, w0_ref, w1_ref, w2_ref, t0_ref, t1_ref, t2_ref,
                  b0_ref, b1_ref, b2_ref, x0_ref, z_ref, h_ref, u_ref, *, tm, seq):
    @pl.when(pl.program_id(1) == 0)
    def _():
        _fill_normed(h_ref, xp_ref, x_ref, xn_ref, g_ref[...], tm)

    h = h_ref[...]
    inside = _in_sequence(tm, seq)

    def branch(w_ref, t_ref, b_ref):
        u_ref[...] = jnp.where(inside, _dot(h, w_ref[...]), 0.0)
        return _dwconv(u_ref, t_ref[...], tm) + b_ref[...]

    x0_ref[...] = branch(w0_ref, t0_ref, b0_ref).astype(BF)
    x1 = branch(w1_ref, t1_ref, b1_ref)
    z_ref[...] = (branch(w2_ref, t2_ref, b2_ref) * x1).astype(BF)


def _hyena_in(x, g, w_in, taps, bias, layer, j, *, tm, tn):
    seq, d = x.shape
    nd = d // tn
    kw = taps.shape[1]
    col = lambda k: (lambda i, c: (j, 0, k * nd + c))
    return pl.pallas_call(
        functools.partial(_hy_in_kernel, tm=tm, seq=seq),
        grid=(seq // tm, nd),
        in_specs=_halo_specs(tm, d, seq)
        + [pl.BlockSpec((None, 1, d), lambda i, c: (layer, 0, 0))]
        + [pl.BlockSpec((None, d, tn), col(k)) for k in range(3)]
        + [pl.BlockSpec((None, kw, tn), col(k)) for k in range(3)]
        + [pl.BlockSpec((None, 1, tn), col(k)) for k in range(3)],
        out_specs=[pl.BlockSpec((tm, tn), lambda i, c: (i, c))] * 2,
        out_shape=[jax.ShapeDtypeStruct((seq, d), BF)] * 2,
        scratch_shapes=[pltpu.VMEM((tm + 2 * HALO, d), BF),
                        pltpu.VMEM((tm + 2 * HALO, tn), F32)],
        compiler_params=_params(40),
        name="hyena_in",
    )(x, x, x, g, w_in, w_in, w_in, taps, taps, taps, bias, bias, bias)


GROUP = V7X_BF16_SUBLANES


def _dft_plan(seq):
    n = 2 * seq
    n2 = 256 if n >= 256 * 32 else 64
    n1 = n // n2
    h1 = n1 // 2
    assert h1 & (h1 - 1) == 0 and h1 >= V7X_BF16_SUBLANES, "sequence length must be a power of two"
    k1p = _round_up(h1 + 1, V7X_BF16_SUBLANES)
    pack = max(1, V7X_LANES // h1)
    return n, n1, n2, h1, k1p, pack


def _dft_tables(seq):
    n, n1, n2, h1, k1p, pack = _dft_plan(seq)
    khalf = _round_up(k1p, V7X_LANES)
    two_pi = 2.0 * math.pi

    k1 = np.arange(k1p)
    pos = np.arange(h1)[None, :] * n2 + np.arange(n2)[:, None]
    ang = ((k1[None, :, None] * pos[:, None, :]) % n) * (two_pi / n)
    fwd = np.concatenate([np.cos(ang), -np.sin(ang)], axis=1)
    fwd = fwd.reshape(n2 // pack, pack, 2 * k1p, 1, h1) * np.eye(pack)[None, :, None, :, None]
    fwd = fwd.reshape(n2 // pack, pack * 2 * k1p, pack * h1)

    weight = np.where(k1 == 0, 1.0, np.where(k1 < n1 // 2, 2.0, np.where(k1 == n1 // 2, 1.0, 0.0))) / n
    angt = np.swapaxes(ang, 1, 2)
    widen = ((0, 0), (0, 0), (0, khalf - k1p))
    inv = np.concatenate([np.pad(np.cos(angt) * weight, widen),
                          np.pad(-np.sin(angt) * weight, widen)], axis=2)

    idx = np.arange(n2)
    a2 = ((idx[:, None] * idx[None, :]) % n2) * (two_pi / n2)
    cr, ci = np.cos(a2), -np.sin(a2)
    mid_fwd = np.block([[cr, -ci], [ci, cr]])
    mid_inv = np.block([[cr, ci], [-ci, cr]])
    return tuple(jnp.asarray(t.astype(BF)) for t in (fwd, inv, mid_fwd, mid_inv))


def _group_positions(step, h1, n2, first=0, rows=None):
    rows = GROUP * h1 if rows is None else rows
    e = first + lax.broadcasted_iota(jnp.int32, (rows, 1), 0)
    q = lax.shift_right_logical(e, jnp.int32(int(math.log2(h1))))
    return (e - q * h1) * n2 + step * GROUP + q


def _phase_tables(seq, bands, width):
    n, n1, n2, h1, k1p, pack = _dft_plan(seq)
    f = np.linspace(1e-4, bands - 1, bands, dtype=np.float32).astype(np.float64)
    lanes = np.zeros((width,))
    lanes[1:bands + 1] = f
    lanes[bands + 1:2 * bands + 1] = f
    major = (2.0 * np.pi / seq) * (np.arange(h1) * n2)[:, None] * lanes[None, :]
    minor = (2.0 * np.pi / seq) * np.arange(n2)[:, None] * lanes[None, :]
    tile = lambda a: np.broadcast_to(a[:, None, :], (n2, V7X_SUBLANES, width))
    tabs = (np.cos(major), np.sin(major), tile(np.cos(minor)), tile(np.sin(minor)))
    return tuple(jnp.asarray(np.ascontiguousarray(a), dtype=F32) for a in tabs)


def _hy_mlp_kernel(ca_ref, sa_ref, cb_ref, sb_ref, w1_ref, b1_ref, w2_ref, b2_ref, w3_ref, b3_ref, fr_ref, a_ref,
                   *, h1, n2, seq, bands):
    rows, width = GROUP * h1, ca_ref.shape[1]
    t = _group_positions(pl.program_id(0), h1, n2).astype(F32) / (seq - 1)
    split = lambda r: r[...].reshape(1, h1 // V7X_SUBLANES, V7X_SUBLANES, width)
    ca, sa = split(ca_ref), split(sa_ref)
    cb, sb = cb_ref[...][:, None], sb_ref[...][:, None]
    cos = (ca * cb - sa * sb).reshape(rows, width)
    sin = (sa * cb + ca * sb).reshape(rows, width)
    lane = lax.broadcasted_iota(jnp.int32, (1, width), 1)
    feat = jnp.where(lane == 0, t,
                     jnp.where(lane <= bands, cos, jnp.where(lane <= 2 * bands, -sin, 0.0)))
    fr = fr_ref[...]
    copies = w1_ref.shape[0]
    part = rows // copies
    pre = _dot_hi(feat[0:part], w1_ref[0])
    for k in range(1, copies):
        pre = pre + _dot_hi(feat[k * part:(k + 1) * part], w1_ref[k])
    a = jnp.sin(fr * (pre + b1_ref[...]))
    a = jnp.sin(fr * (_dot_hi(a, w2_ref[...]) + b2_ref[...]))
    a = jnp.sin(fr * (_dot_hi(a, w3_ref[...]) + b3_ref[...]))
    a_ref[pl.ds(0, part), :] = a.astype(BF)
    for k in range(1, copies):
        hidden = a.shape[1]
        a_ref[pl.ds(k * part, part), :] = pltpu.roll(a, hidden - k * (hidden // copies), axis=1).astype(BF)


def _hyena_filter_features(seq, w1, b1, w2, b2, w3, b3, fr, bands):
    n, n1, n2, h1, k1p, pack = _dft_plan(seq)
    wp = w1.shape[2]
    ca, sa, cb, sb = _phase_tables(seq, bands, w1.shape[1])
    full = lambda a: pl.BlockSpec(a.shape, lambda s: (0,) * a.ndim)
    group = pl.BlockSpec((GROUP,) + cb.shape[1:], lambda s: (s, 0, 0))
    args = (ca, sa, cb, sb, w1, b1, w2, b2, w3, b3, fr)
    return pl.pallas_call(
        functools.partial(_hy_mlp_kernel, h1=h1, n2=n2, seq=seq, bands=bands),
        grid=(n2 // GROUP,),
        in_specs=[full(ca), full(sa), group, group] + [full(a) for a in args[4:]],
        out_specs=pl.BlockSpec((GROUP * h1, wp), lambda s: (s, 0)),
        out_shape=jax.ShapeDtypeStruct((seq, wp), BF),
        compiler_params=_params(32),
        name="hyena_filter_mlp",
    )(*args)


def _dft_a_kernel(tab_ref, z_ref, a3_ref, w4f_ref, w4b_ref, dl_ref, az_ref, af_ref, ab_ref, ssq_ref,
                  zt_ref, stage_ref, *, h1, n2, k1p, pack, seq):
    step = pl.program_id(1)
    cw = z_ref.shape[-1]
    kin = pack * h1

    @pl.when(step == 0)
    def _():
        ssq_ref[...] = jnp.zeros_like(ssq_ref)

    zt_ref[...] = pltpu.einshape("abc->bac", z_ref[...]).reshape(GROUP * h1, cw)
    ssq = jnp.zeros((1, cw), F32)
    for p in range(GROUP // pack):
        rows = pl.ds(p * kin, kin)
        pos = _group_positions(step, h1, n2, p * kin, kin)
        decay = jnp.exp(-(pos.astype(F32) / (seq - 1)) * dl_ref[...])
        a3 = a3_ref[rows, :]
        hf = _dot(a3, w4f_ref[...]) * decay
        hb = jnp.where(pos == 0, 0.0, _dot(a3, w4b_ref[...]) * decay)
        ssq = ssq + jnp.sum(hf * hf + hb * hb, axis=0, keepdims=True)
        tab = tab_ref[p]
        for s, src in enumerate((zt_ref[rows, :], hf.astype(BF), hb.astype(BF))):
            a = _dot(tab, src)
            stage_ref[s, pl.ds(p * pack, pack)] = a.astype(BF).reshape(pack, 2 * k1p, cw)
    ssq_ref[...] += ssq
    for s, dst in enumerate((az_ref, af_ref, ab_ref)):
        dst[...] = pltpu.einshape("qkc->kqc", stage_ref[s])


def _dft_stage_a(tab, z, a3, w4, deltas, seq, d, *, cw):
    n, n1, n2, h1, k1p, pack = _dft_plan(seq)
    wp = a3.shape[1]
    nc = d // cw
    spectrum = pl.BlockSpec((2 * k1p, GROUP, cw), lambda c, s: (0, s, c))
    shape = jax.ShapeDtypeStruct((2 * k1p, n2, d), BF)
    return pl.pallas_call(
        functools.partial(_dft_a_kernel, h1=h1, n2=n2, k1p=k1p, pack=pack, seq=seq),
        grid=(nc, n2 // GROUP),
        in_specs=[pl.BlockSpec((GROUP // pack,) + tab.shape[1:], lambda c, s: (s, 0, 0)),
                  pl.BlockSpec((h1, GROUP, cw), lambda c, s: (0, s, c)),
                  pl.BlockSpec((GROUP * h1, wp), lambda c, s: (s, 0)),
                  pl.BlockSpec((wp, cw), lambda c, s: (0, c)),
                  pl.BlockSpec((wp, cw), lambda c, s: (0, nc + c)),
                  pl.BlockSpec((1, cw), lambda c, s: (0, c))],
        out_specs=[spectrum] * 3 + [pl.BlockSpec((1, cw), lambda c, s: (0, c))],
        out_shape=[shape] * 3 + [jax.ShapeDtypeStruct((1, d), F32)],
        scratch_shapes=[pltpu.VMEM((GROUP * h1, cw), BF),
                        pltpu.VMEM((3, GROUP, 2 * k1p, cw), BF)],
        compiler_params=_params(48),
        name="dft_stage_a",
    )(tab, z.reshape(h1, n2, d), a3, w4, w4, deltas)


def _dft_b_kernel(mf_ref, mi_ref, az_ref, af_ref, ab_ref, s_ref, *, n2, d, cw, nyquist):
    @pl.when(pl.program_id(0) > nyquist)
    def _():
        s_ref[...] = jnp.zeros_like(s_ref)

    @pl.when(pl.program_id(0) <= nyquist)
    def _():
        mf = mf_ref[...]
        mi = mi_ref[...]
        def forward(c):
            cols = pl.ds(c * cw, cw)
            return [_dot(mf, r[:, 0, :, cols].reshape(2 * n2, cw)) for r in (az_ref, af_ref, ab_ref)]

        ahead = forward(0)
        for c in range(d // cw):
            cols = pl.ds(c * cw, cw)
            xz, xf, xb = ahead
            if c + 1 < d // cw:
                ahead = forward(c + 1)
            kr = xf[:n2] + xb[:n2]
            ki = xf[n2:] - xb[n2:]
            yr = xz[:n2] * kr - xz[n2:] * ki
            yi = xz[:n2] * ki + xz[n2:] * kr
            s = _dot(mi, jnp.concatenate([yr, yi], axis=0).astype(BF))
            s_ref[0, 0, :, cols] = s[:n2].astype(BF)
            s_ref[1, 0, :, cols] = s[n2:].astype(BF)


def _dft_stage_b(mid_fwd, mid_inv, az, af, ab, seq, d, *, cw):
    n, n1, n2, h1, k1p, pack = _dft_plan(seq)
    view = lambda a: a.reshape(2, k1p, n2, d)
    blk = pl.BlockSpec((2, 1, n2, d), lambda k: (0, k, 0, 0))
    src = pl.BlockSpec((2, 1, n2, d), lambda k: (0, jnp.minimum(k, h1), 0, 0))
    mat = _resident((2 * n2, 2 * n2), lambda k: (0, 0))
    return pl.pallas_call(
        functools.partial(_dft_b_kernel, n2=n2, d=d, cw=cw, nyquist=h1),
        grid=(k1p,),
        in_specs=[mat, mat, src, src, src],
        out_specs=blk,
        out_shape=jax.ShapeDtypeStruct((2, k1p, n2, d), BF),
        compiler_params=_params(48),
        name="dft_stage_b",
    )(mid_fwd, mid_inv, view(az), view(af), view(ab))


def _dft_c_kernel(tab_ref, s_ref, y_ref, pad_ref, st_ref, ys_ref, *, k1p):
    khalf = pad_ref.shape[0] // 2

    @pl.when(jnp.logical_and(pl.program_id(0) == 0, pl.program_id(1) == 0))
    def _():
        pad_ref[...] = jnp.zeros_like(pad_ref)

    st_ref[...] = pltpu.einshape("kqc->qkc", s_ref[...])
    for q in range(GROUP):
        pad_ref[pl.ds(0, k1p), :] = st_ref[q, pl.ds(0, k1p), :]
        pad_ref[pl.ds(khalf, k1p), :] = st_ref[q, pl.ds(k1p, k1p), :]
        ys_ref[q] = _dot(tab_ref[q], pad_ref[...]).astype(BF)
    y_ref[...] = pltpu.einshape("qnc->nqc", ys_ref[...])


def _dft_stage_c(tab, s, seq, d, *, cw):
    n, n1, n2, h1, k1p, pack = _dft_plan(seq)
    kc = tab.shape[2]
    y = pl.pallas_call(
        functools.partial(_dft_c_kernel, k1p=k1p),
        grid=(d // cw, n2 // GROUP),
        in_specs=[pl.BlockSpec((GROUP, h1, kc), lambda c, q: (q, 0, 0)),
                  pl.BlockSpec((2 * k1p, GROUP, cw), lambda c, q: (0, q, c))],
        out_specs=pl.BlockSpec((h1, GROUP, cw), lambda c, q: (0, q, c)),
        out_shape=jax.ShapeDtypeStruct((h1, n2, d), BF),
        scratch_shapes=[pltpu.VMEM((kc, cw), BF),
                        pltpu.VMEM((GROUP, 2 * k1p, cw), BF),
                        pltpu.VMEM((GROUP, h1, cw), BF)],
        compiler_params=_params(40),
        name="dft_stage_c",
    )(tab, s.reshape(2 * k1p, n2, d))
    return y.reshape(seq, d)


def _hy_out_kernel(x_ref, y_ref, z_ref, x0_ref, ssq_ref, skip_ref, wo_ref, o_ref):
    y = y_ref[...].astype(F32) * lax.rsqrt(ssq_ref[...]) + z_ref[...].astype(F32) * skip_ref[...]
    gated = (y * x0_ref[...].astype(F32)).astype(BF)
    o_ref[...] = x_ref[...] + _dot(gated, wo_ref[...])


def _hyena_out(x, y, z, x0, ssq, skip, w_out, j, *, tm):
    seq, d = x.shape
    tile = pl.BlockSpec((tm, d), lambda i: (i, 0))
    return pl.pallas_call(
        _hy_out_kernel,
        grid=(seq // tm,),
        in_specs=[tile, tile, tile, tile,
                  pl.BlockSpec((1, d), lambda i: (0, 0)),
                  pl.BlockSpec((None, 1, d), lambda i: (j, 0, 0)),
                  _resident((None, d, d), lambda i: (j, 0, 0))],
        out_specs=tile,
        out_shape=jax.ShapeDtypeStruct((seq, d), F32),
        compiler_params=_params(40),
        name="hyena_out",
    )(x, y, z, x0, ssq, skip, w_out)


def _hyena_mixer(x, g, w_in, taps, bias, f_w1, f_b1, f_w2, f_b2, f_w3, f_b3, f_w4, freq, skip, w_out,
                 layer, j, *, tm, tn):
    seq, d = x.shape
    x0, z = _hyena_in(x, g, w_in, taps, bias, layer, j, tm=tm, tn=tn)

    emb, order = f_w1.shape[1], f_w1.shape[2]
    bands = (emb - 1) // 2
    wp = _round_up(order, V7X_LANES)
    ep = _round_up(emb, V7X_LANES)
    pad2 = lambda a, r, c: jnp.pad(a.astype(F32), ((0, r - a.shape[0]), (0, c - a.shape[1])))
    deltas = jnp.abs(jnp.linspace(math.log(HY_TARGET) / HY_FAST_DECAY_PCT,
                                  math.log(HY_TARGET) / HY_SLOW_DECAY_PCT, d, dtype=F32))[None, :]
    copies = wp // order if wp % order == 0 else 1
    lanes = lambda v: pad2(jnp.tile(v[None, :], (1, copies)), 1, wp)
    diag = lambda w: pad2(jnp.kron(jnp.eye(copies, dtype=F32), w.astype(F32)), wp, wp)
    first = jnp.stack([pad2(jnp.pad(f_w1[j].astype(F32), ((0, 0), (k * order, 0))), ep, wp) for k in range(copies)])
    a3 = _hyena_filter_features(
        seq, first, lanes(f_b1[j]), diag(f_w2[j]), lanes(f_b2[j]), diag(f_w3[j]), lanes(f_b3[j]),
        lanes(freq[j]), bands)

    cw = min(512, d)
    tab_a, tab_c, mid_fwd, mid_inv = _dft_tables(seq)
    az, af, ab, ssq = _dft_stage_a(tab_a, z, a3, pad2(f_w4[j], wp, 2 * d).astype(BF), deltas, seq, d, cw=cw)
    s = _dft_stage_b(mid_fwd, mid_inv, az, af, ab, seq, d, cw=cw)
    y = _dft_stage_c(tab_c, s, seq, d, cw=cw)
    return _hyena_out(x, y, z, x0, ssq, skip, w_out, j, tm=min(256, seq))


def kernel(x, p, norm_ffn1, ffn1_w_gate, ffn1_w_up, ffn1_w_down, norm_mix, norm_ffn2, ffn2_w_gate, ffn2_w_up, ffn2_w_down, norm_ple, ple_w_gate, ple_w_up, sc_w_in, sc_conv, sc_w_out, cf_w_pw1, cf_b_pw1, cf_dw, cf_b_dw, cf_ln_g, cf_ln_b, cf_w_pw2, cf_b_pw2, hy_w_in, hy_short, hy_b_short, hy_f_w1, hy_f_b1, hy_f_w2, hy_f_b2, hy_f_w3, hy_f_b3, hy_f_w4, hy_freq, hy_skip, hy_w_out, norm_final):
    batch, seq, d = x.shape
    depth = p.shape[0]
    assert batch == 1, "the row tiling treats the sequence as the only token axis"
    tm = min(512, seq)
    tn = min(512, d)
    tn_cf = min(256, d)
    tf = min(512, ffn1_w_gate.shape[-1])
    tm_ffn = min(1024, seq)

    bf = lambda a: a.astype(BF)
    row = lambda a: a[:, None, :].astype(F32)
    ffn1_w = (ffn1_w_gate, ffn1_w_up, ffn1_w_down)
    ffn2_w = (ffn2_w_gate, ffn2_w_up, ffn2_w_down)
    w_cur = tuple(bf(w[0]) for w in ffn1_w)
    n_ffn1, n_mix, n_ffn2, n_ple = row(norm_ffn1), row(norm_mix), row(norm_ffn2), row(norm_ple)
    ple_gate, ple_up = bf(ple_w_gate), bf(ple_w_up)
    sc_in, sc_out = bf(sc_w_in), bf(sc_w_out)
    cf_w1, cf_w2 = bf(cf_w_pw1), bf(cf_w_pw2)
    hy_in, hy_out = bf(hy_w_in), bf(hy_w_out)
    gf = norm_final[None, :].astype(F32)

    xs = x.reshape(seq, d)
    ps = p.reshape(depth, seq, p.shape[-1])
    for i in range(depth):
        m, j = i % 3, i // 3
        xs, w_nxt = _ffn(xs, n_ffn1, *w_cur, i, ffn2_w + (i,), tm=tm_ffn, tf=tf)
        if m == 0:
            xs = _short_conv_mixer(xs, n_mix, sc_in, sc_conv.astype(F32), sc_out, i, j, tm=tm, tn=tn)
        elif m == 1:
            cf_taps = jnp.broadcast_to(cf_dw.astype(F32)[:, :, None, :], cf_dw.shape[:2] + (V7X_SUBLANES, d))
            xs = _conformer_mixer(xs, n_mix, cf_w1, row(cf_b_pw1), cf_taps, row(cf_b_dw),
                                  row(cf_ln_g), row(cf_ln_b), cf_w2, row(cf_b_pw2), i, j, tm=tm, tn=tn_cf)
        else:
            xs = _hyena_mixer(xs, n_mix, hy_in, hy_short.astype(F32), row(hy_b_short),
                              hy_f_w1, hy_f_b1, hy_f_w2, hy_f_b2, hy_f_w3, hy_f_b3, hy_f_w4,
                              hy_freq, row(hy_skip), hy_out, i, j, tm=tm, tn=tn)
        xs, w_cur = _ffn(xs, n_ffn2, *w_nxt, i, ffn1_w + (i + 1,) if i + 1 < depth else None, tm=tm_ffn, tf=tf)
        xs = _ple(xs, ps, n_ple, ple_gate, ple_up, gf, i, tm=tm, final=(i == depth - 1))
    return xs.reshape(batch, seq, d)
```
